```python
import math
import jax, jax.numpy as jnp
from jax import lax
import numpy as np

D_MODEL = 1024
BATCH = 8
SEQ = 4096
DEPTH = 2

N_MIXERS = 2
EPS = 1e-6
CONV_WIDTH = 3
WINDOWS = (128, 512, 2048)
DILATIONS = (1, 4, 16)
N_GROUPS = 3
HEADS_PER_GROUP = 8
HEAD_DIM = 64
GROUP_WIDTH = HEADS_PER_GROUP * HEAD_DIM
N_ATTN_HEADS = N_GROUPS * HEADS_PER_GROUP
QKV_WIDTH = 3 * N_GROUPS * GROUP_WIDTH
Q_BLOCK = 128
NEG_INF = -1e30
N_BUCKETS = 32
MAX_DISTANCE = 1024
D_FF = 2816
N_A = (DEPTH + N_MIXERS - 1) // N_MIXERS
N_B = DEPTH // N_MIXERS

kernel_name = "hybrid_shortconv_dilated_attn_encoder"


def rmsnorm(x, w):
    xf = x.astype(jnp.float32)
    y = xf * lax.rsqrt(jnp.mean(xf * xf, axis=-1, keepdims=True) + EPS)
    return (y * w.astype(jnp.float32)).astype(x.dtype)


def dwconv3(x, w):
    return lax.conv_general_dilated(
        x, w[:, None, :].astype(x.dtype), window_strides=(1,),
        padding=[(CONV_WIDTH // 2, CONV_WIDTH // 2)],
        dimension_numbers=("NWC", "WIO", "NWC"),
        feature_group_count=x.shape[-1])


def t5_bucket(rel):
    half = N_BUCKETS // 2
    max_exact = half // 2
    n = jnp.abs(rel)
    side = jnp.where(rel > 0, half, 0)
    nf = jnp.maximum(n, 1).astype(jnp.float32)
    large = max_exact + (jnp.log(nf / max_exact) / math.log(MAX_DISTANCE / max_exact)
                         * (half - max_exact)).astype(jnp.int32)
    large = jnp.minimum(large, half - 1)
    return side + jnp.where(n < max_exact, n, large)


def short_conv_mixer(xn, w_in, conv_w, w_out):
    b_g, c_g, h = jnp.split(xn @ w_in, 3, axis=-1)
    return (b_g * dwconv3(c_g * h, conv_w)) @ w_out


def dilated_attention(xn, w_qkv, w_out, rel_bias):
    B_, S_, _ = xn.shape
    qkv = (xn @ w_qkv).reshape(B_, S_, 3, N_GROUPS, HEADS_PER_GROUP, HEAD_DIM)
    q = qkv[:, :, 0] * (HEAD_DIM ** -0.5)
    k = qkv[:, :, 1]
    v = qkv[:, :, 2]
    qs = [q[:, :, g] for g in range(N_GROUPS)]
    ks = [k[:, :, g] for g in range(N_GROUPS)]
    vs = [v[:, :, g] for g in range(N_GROUPS)]
    offs, biases = [], []
    for g in range(N_GROUPS):
        half = WINDOWS[g] // (2 * DILATIONS[g])
        o_g = DILATIONS[g] * jnp.arange(-half, half + 1, dtype=jnp.int32)
        offs.append(o_g)
        cols = rel_bias[t5_bucket(o_g)][:, g * HEADS_PER_GROUP:(g + 1) * HEADS_PER_GROUP]
        biases.append(cols.astype(jnp.float32).T[None, :, None, :])

    def block(q0):
        qpos = q0 + jnp.arange(Q_BLOCK, dtype=jnp.int32)
        outs, lses = [], []
        for g in range(N_GROUPS):
            qg = lax.dynamic_slice_in_dim(qs[g], q0, Q_BLOCK, axis=1)
            kpos = qpos[:, None] + offs[g][None, :]
            valid = (kpos >= 0) & (kpos < S_)
            kidx = jnp.clip(kpos, 0, S_ - 1)
            kg = jnp.take(ks[g], kidx, axis=1)
            vg = jnp.take(vs[g], kidx, axis=1)
            s = jnp.einsum("bqhd,bqkhd->bhqk", qg, kg,
                           preferred_element_type=jnp.float32) + biases[g]
            s = jnp.where(valid[None, None], s, NEG_INF)
            m = jnp.max(s, axis=-1, keepdims=True)
            e = jnp.exp(s - m)
            den = jnp.sum(e, axis=-1)
            o = jnp.einsum("bhqk,bqkhd->bqhd", e, vg.astype(jnp.float32))
            outs.append(o / jnp.transpose(den, (0, 2, 1))[..., None])
            lses.append(m[..., 0] + jnp.log(den))
        alpha = jax.nn.softmax(jnp.stack(lses), axis=0)
        o = jnp.einsum("gbhq,gbqhd->bqhd", alpha, jnp.stack(outs))
        return o.astype(xn.dtype)

    starts = jnp.arange(S_ // Q_BLOCK, dtype=jnp.int32) * Q_BLOCK
    o = lax.map(block, starts)
    o = jnp.moveaxis(o, 0, 1).reshape(B_, S_, GROUP_WIDTH)
    return o @ w_out


def conv_ffn(xn, w_up, conv_w, conv_b, w_down):
    g, u = jnp.split(xn @ w_up, 2, axis=-1)
    g = dwconv3(g, conv_w) + conv_b
    return (jax.nn.silu(g) * u) @ w_down


def setup_inputs(seed: int = 0) -> dict:
    key = jax.random.key(seed)
    ks = jax.random.split(key, 14)
    f32 = jnp.float32
    D, F = D_MODEL, D_FF
    nrm = lambda k, shape, s: jax.random.normal(k, shape, f32) * s
    return {
        "x": nrm(ks[0], (BATCH, SEQ, D), 1.0),
        "norm_w": 1.0 + nrm(ks[1], (DEPTH, 2, D), 0.02),
        "conv_in": nrm(ks[2], (N_A, D, 3 * D), D ** -0.5),
        "conv_w": nrm(ks[3], (N_A, CONV_WIDTH, D), CONV_WIDTH ** -0.5),
        "conv_out": nrm(ks[4], (N_A, D, D), D ** -0.5),
        "attn_qkv": nrm(ks[5], (N_B, D, QKV_WIDTH), D ** -0.5),
        "attn_out": nrm(ks[6], (N_B, GROUP_WIDTH, D), GROUP_WIDTH ** -0.5),
        "rel_bias": nrm(ks[7], (N_BUCKETS, N_ATTN_HEADS), 0.2),
        "ffn_up": nrm(ks[8], (DEPTH, D, 2 * F), D ** -0.5),
        "ffn_conv_w": nrm(ks[9], (DEPTH, CONV_WIDTH, F), CONV_WIDTH ** -0.5),
        "ffn_conv_b": nrm(ks[10], (DEPTH, F), 0.02),
        "ffn_down": nrm(ks[11], (DEPTH, F, D), F ** -0.5),
        "final_norm": 1.0 + nrm(ks[12], (D,), 0.02),
    }


def reference(x, norm_w, conv_in, conv_w, conv_out, attn_qkv, attn_out, rel_bias,
              ffn_up, ffn_conv_w, ffn_conv_b, ffn_down, final_norm):
    for i in range(DEPTH):
        j = i // N_MIXERS
        h = rmsnorm(x, norm_w[i, 0])
        if i % N_MIXERS == 0:
            x = x + short_conv_mixer(h, conv_in[j], conv_w[j], conv_out[j])
        else:
            x = x + dilated_attention(h, attn_qkv[j], attn_out[j], rel_bias)
        h = rmsnorm(x, norm_w[i, 1])
        x = x + conv_ffn(h, ffn_up[i], ffn_conv_w[i], ffn_conv_b[i], ffn_down[i])
    return rmsnorm(x, final_norm)
```

```python
import functools
import math

import jax
import jax.numpy as jnp
import numpy as np
from jax import lax
from jax.experimental import pallas as pl
from jax.experimental.pallas import tpu as pltpu

D_MODEL = 1024
D_FF = 2816
EPS = 1e-6
DILATIONS = (1, 4, 16)
WINDOWS = (128, 512, 2048)
N_GROUPS = 3
HEADS_PER_GROUP = 8
HEAD_DIM = 64
GROUP_WIDTH = HEADS_PER_GROUP * HEAD_DIM
N_BUCKETS = 32
MAX_DISTANCE = 1024
NEG_INF = -1e30

HALF_KEYS = 64
LANES = 128
HALO = 16
ROW_TILE = 512
COL_CHUNK = 256
Q_TILE = 128
K_TILE = Q_TILE + 2 * HALF_KEYS
VMEM_LIMIT = 56 * 1024 * 1024

BF16 = jnp.bfloat16
F32 = jnp.float32


def _dot(a, b):
    return jnp.dot(a, b, preferred_element_type=F32)


def _rms(x, w):
    return x * lax.rsqrt(jnp.mean(x * x, axis=-1, keepdims=True) + EPS) * w


def _resident(shape):
    zeros = (0,) * len(shape)
    return pl.BlockSpec(shape, lambda *_: zeros, pipeline_mode=pl.Buffered(1))


def _seq_specs(seq, tile):
    per = tile // HALO
    last = seq // HALO - 1
    main = pl.BlockSpec((None, tile, D_MODEL), lambda b, i: (b, i, 0))
    prev = pl.BlockSpec((None, HALO, D_MODEL),
                        lambda b, i: (b, jnp.maximum(i * per - 1, 0), 0))
    nxt = pl.BlockSpec((None, HALO, D_MODEL),
                       lambda b, i: (b, jnp.minimum((i + 1) * per, last), 0))
    return main, prev, nxt


def _fill_normed(hn_ref, xp_ref, x_ref, xn_ref, nw):
    tile = x_ref.shape[0]
    hn_ref[0:HALO, :] = _rms(xp_ref[...], nw).astype(BF16)
    hn_ref[HALO:HALO + tile, :] = _rms(x_ref[...], nw).astype(BF16)
    hn_ref[HALO + tile:, :] = _rms(xn_ref[...], nw).astype(BF16)


def _conv3(p, cw_ref, cols, tile):
    i = pl.program_id(1)
    rows = lax.broadcasted_iota(jnp.int32, (p.shape[0], 1), 0)
    outside = ((i == 0) & (rows < HALO)) | (
        (i == pl.num_programs(1) - 1) & (rows >= HALO + tile))
    p = jnp.where(outside, 0.0, p)
    n = p.shape[0]
    before = pltpu.roll(p, 1, axis=0)[HALO:HALO + tile]
    after = pltpu.roll(p, n - 1, axis=0)[HALO:HALO + tile]
    mid = p[HALO:HALO + tile]
    return (cw_ref[0:1, cols] * before + cw_ref[1:2, cols] * mid
            + cw_ref[2:3, cols] * after)


def _mixer_kernel(x_ref, xp_ref, xn_ref, nw_ref, win_ref, cw_ref, wout_ref,
                  o_ref, hn_ref, a_ref):
    tile = x_ref.shape[0]
    d = D_MODEL
    _fill_normed(hn_ref, xp_ref, x_ref, xn_ref, nw_ref[...])
    for j in range(d // COL_CHUNK):
        cols = slice(j * COL_CHUNK, (j + 1) * COL_CHUNK)
        gate = _dot(hn_ref[HALO:HALO + tile, :], win_ref[:, cols])
        c = _dot(hn_ref[...], win_ref[:, d + j * COL_CHUNK:d + (j + 1) * COL_CHUNK])
        h = _dot(hn_ref[...], win_ref[:, 2 * d + j * COL_CHUNK:2 * d + (j + 1) * COL_CHUNK])
        conv = _conv3(c * h, cw_ref, cols, tile)
        a_ref[:, cols] = (gate * conv).astype(BF16)
    o_ref[...] = x_ref[...] + _dot(a_ref[...], wout_ref[...])


def _ffn_kernel(x_ref, xp_ref, xn_ref, nw_ref, wup_ref, cw_ref, cb_ref, wdown_ref,
                fw_ref, o_ref, hn_ref, a_ref, *, final_norm):
    tile = x_ref.shape[0]
    f = D_FF
    _fill_normed(hn_ref, xp_ref, x_ref, xn_ref, nw_ref[...])
    for j in range(f // COL_CHUNK):
        cols = slice(j * COL_CHUNK, (j + 1) * COL_CHUNK)
        g = _dot(hn_ref[...], wup_ref[:, cols])
        u = _dot(hn_ref[HALO:HALO + tile, :],
                 wup_ref[:, f + j * COL_CHUNK:f + (j + 1) * COL_CHUNK])
        g = _conv3(g, cw_ref, cols, tile) + cb_ref[:, cols]
        act = g / (1.0 + jnp.exp(-g))
        a_ref[:, cols] = (act * u).astype(BF16)
    y = x_ref[...] + _dot(a_ref[...], wdown_ref[...])
    if final_norm:
        y = _rms(y, fw_ref[...])
    o_ref[...] = y


def _qkv_kernel(x_ref, nw_ref, w_ref, o0_ref, o1_ref, o2_ref, hn_ref, hp_ref):
    tile = x_ref.shape[0]
    gw3 = 3 * GROUP_WIDTH
    hn = _rms(x_ref[...], nw_ref[...])
    for c in range(D_MODEL // LANES):
        hn_ref[c] = hn[:, c * LANES:(c + 1) * LANES]
    for g, (r, o_ref) in enumerate(zip(DILATIONS, (o0_ref, o1_ref, o2_ref))):
        rows = tile // r
        for p in range(r):
            for c in range(D_MODEL // LANES):
                hp_ref[p * rows:(p + 1) * rows, c * LANES:(c + 1) * LANES] = (
                    hn_ref[c, pl.ds(p, rows, stride=r), :].astype(BF16))
        res = _dot(hp_ref[...], w_ref[:, g * gw3:(g + 1) * gw3])
        for p in range(r):
            blk = res[p * rows:(p + 1) * rows]
            o_ref[p, :, 0:GROUP_WIDTH] = (
                blk[:, 0:GROUP_WIDTH] * (HEAD_DIM ** -0.5)).astype(BF16)
            o_ref[p, :, GROUP_WIDTH:] = blk[:, GROUP_WIDTH:].astype(BF16)


def _attn_kernel(q_ref, k_ref, v_ref, bias_ref, o_ref, lse_ref, *, phase_len):
    t = pl.program_id(2)
    start = jnp.clip(t * Q_TILE - HALF_KEYS, 0, phase_len - K_TILE)
    start = pl.multiple_of(start, HALF_KEYS)
    variant = (t * Q_TILE - start) // HALF_KEYS
    lane = lax.broadcasted_iota(jnp.int32, (1, LANES), 1)
    low = lane < HEAD_DIM
    for hp in range(HEADS_PER_GROUP // 2):
        cols = slice(hp * LANES, (hp + 1) * LANES)
        q2 = q_ref[:, cols]
        k2 = k_ref[pl.ds(start, K_TILE), cols]
        v2 = v_ref[pl.ds(start, K_TILE), cols]
        zero = jnp.zeros_like(q2)
        outs, lses = [], []
        for half, qh in enumerate((jnp.where(low, q2, zero), jnp.where(low, zero, q2))):
            s = lax.dot_general(qh, k2, (((1,), (1,)), ((), ())),
                                preferred_element_type=F32)
            s = s + bias_ref[variant, 2 * hp + half]
            m = jnp.max(s, axis=-1, keepdims=True)
            e = jnp.exp(s - m)
            den = jnp.sum(e, axis=-1, keepdims=True)
            outs.append(_dot(e.astype(BF16), v2) / den)
            lses.append(m + jnp.log(den))
        o_ref[:, cols] = jnp.where(low, outs[0], outs[1]).astype(BF16)
        lse_ref[:, cols] = jnp.where(low, lses[0], lses[1])


def _combine_kernel(x_ref, o0_ref, o1_ref, o2_ref, l0_ref, l1_ref, l2_ref, w_ref,
                    out_ref, on_ref, ln_ref, a_ref):
    tile = x_ref.shape[0]
    nchunk = GROUP_WIDTH // LANES
    for g, (r, o_ref, l_ref) in enumerate(
            zip(DILATIONS, (o0_ref, o1_ref, o2_ref), (l0_ref, l1_ref, l2_ref))):
        rows = tile // r
        for p in range(r):
            o = o_ref[p].astype(F32)
            lse = l_ref[p]
            for c in range(nchunk):
                cols = slice(c * LANES, (c + 1) * LANES)
                on_ref[g, c, pl.ds(p, rows, stride=r), :] = o[:, cols]
                ln_ref[g, c, pl.ds(p, rows, stride=r), :] = lse[:, cols]
    for c in range(nchunk):
        l0, l1, l2 = ln_ref[0, c], ln_ref[1, c], ln_ref[2, c]
        m = jnp.maximum(jnp.maximum(l0, l1), l2)
        e0, e1, e2 = jnp.exp(l0 - m), jnp.exp(l1 - m), jnp.exp(l2 - m)
        o = (e0 * on_ref[0, c] + e1 * on_ref[1, c] + e2 * on_ref[2, c]) / (e0 + e1 + e2)
        a_ref[:, c * LANES:(c + 1) * LANES] = o.astype(BF16)
    out_ref[...] = x_ref[...] + _dot(a_ref[...], w_ref[...])


def _params():
    return pltpu.CompilerParams(
        dimension_semantics=("arbitrary", "arbitrary"), vmem_limit_bytes=VMEM_LIMIT)


def _mixer(x, nw, w_in, cw, w_out):
    b, s, d = x.shape
    main, prev, nxt = _seq_specs(s, ROW_TILE)
    return pl.pallas_call(
        _mixer_kernel,
        grid=(b, s // ROW_TILE),
        in_specs=[main, prev, nxt, _resident((1, d)), _resident((d, 3 * d)),
                  _resident((3, d)), _resident((d, d))],
        out_specs=main,
        out_shape=jax.ShapeDtypeStruct(x.shape, F32),
        scratch_shapes=[pltpu.VMEM((ROW_TILE + 2 * HALO, d), BF16),
                        pltpu.VMEM((ROW_TILE, d), BF16)],
        compiler_params=_params(),
        name="short_conv_mixer",
    )(x, x, x, nw, w_in, cw, w_out)


def _ffn(x, nw, w_up, cw, cb, w_down, fw, final_norm):
    b, s, d = x.shape
    f = D_FF
    main, prev, nxt = _seq_specs(s, ROW_TILE)
    return pl.pallas_call(
        functools.partial(_ffn_kernel, final_norm=final_norm),
        grid=(b, s // ROW_TILE),
        in_specs=[main, prev, nxt, _resident((1, d)), _resident((d, 2 * f)),
                  _resident((3, f)), _resident((1, f)), _resident((f, d)),
                  _resident((1, d))],
        out_specs=main,
        out_shape=jax.ShapeDtypeStruct(x.shape, F32),
        scratch_shapes=[pltpu.VMEM((ROW_TILE + 2 * HALO, d), BF16),
                        pltpu.VMEM((ROW_TILE, f), BF16)],
        compiler_params=_params(),
        name="conv_ffn_final" if final_norm else "conv_ffn",
    )(x, x, x, nw, w_up, cw, cb, w_down, fw)


def _qkv(x, nw, w):
    b, s, d = x.shape
    gw3 = 3 * GROUP_WIDTH
    out_shapes, out_specs = [], []
    for r in DILATIONS:
        out_shapes.append(jax.ShapeDtypeStruct((b, r, s // r, gw3), BF16))
        out_specs.append(pl.BlockSpec((None, r, ROW_TILE // r, gw3),
                                      lambda bi, i: (bi, 0, i, 0)))
    return pl.pallas_call(
        _qkv_kernel,
        grid=(b, s // ROW_TILE),
        in_specs=[pl.BlockSpec((None, ROW_TILE, d), lambda bi, i: (bi, i, 0)),
                  _resident((1, d)), _resident((d, N_GROUPS * gw3))],
        out_specs=out_specs,
        out_shape=out_shapes,
        scratch_shapes=[pltpu.VMEM((d // LANES, ROW_TILE, LANES), F32),
                        pltpu.VMEM((ROW_TILE, d), BF16)],
        compiler_params=_params(),
        name="qkv_proj",
    )(x, nw, w)


def _attention(qkv, bias, r):
    b, _, phase_len, _ = qkv.shape
    gw = GROUP_WIDTH
    q_spec = pl.BlockSpec((None, None, Q_TILE, gw), lambda bi, p, t: (bi, p, t, 0))
    k_spec = pl.BlockSpec((None, None, phase_len, gw), lambda bi, p, t: (bi, p, 0, 1))
    v_spec = pl.BlockSpec((None, None, phase_len, gw), lambda bi, p, t: (bi, p, 0, 2))
    out_spec = pl.BlockSpec((None, None, Q_TILE, gw), lambda bi, p, t: (bi, p, t, 0))
    return pl.pallas_call(
        functools.partial(_attn_kernel, phase_len=phase_len),
        grid=(b, r, phase_len // Q_TILE),
        in_specs=[q_spec, k_spec, v_spec, _resident(bias.shape)],
        out_specs=[out_spec, out_spec],
        out_shape=[jax.ShapeDtypeStruct((b, r, phase_len, gw), BF16),
                   jax.ShapeDtypeStruct((b, r, phase_len, gw), F32)],
        compiler_params=pltpu.CompilerParams(
            dimension_semantics=("arbitrary",) * 3, vmem_limit_bytes=VMEM_LIMIT),
        name=f"local_attn_d{r}",
    )(qkv, qkv, qkv, bias)


def _combine(x, outs, lses, w_out):
    b, s, d = x.shape
    gw = GROUP_WIDTH
    row = pl.BlockSpec((None, ROW_TILE, d), lambda bi, i: (bi, i, 0))
    phase_specs = [pl.BlockSpec((None, r, ROW_TILE // r, gw), lambda bi, i: (bi, 0, i, 0))
                   for r in DILATIONS]
    return pl.pallas_call(
        _combine_kernel,
        grid=(b, s // ROW_TILE),
        in_specs=[row] + phase_specs + phase_specs + [_resident((gw, d))],
        out_specs=row,
        out_shape=jax.ShapeDtypeStruct(x.shape, F32),
        scratch_shapes=[pltpu.VMEM((N_GROUPS, gw // LANES, ROW_TILE, LANES), F32),
                        pltpu.VMEM((N_GROUPS, gw // LANES, ROW_TILE, LANES), F32),
                        pltpu.VMEM((ROW_TILE, gw), BF16)],
        compiler_params=_params(),
        name="attn_combine_out",
    )(x, *outs, *lses, w_out)


def _t5_bucket(rel):
    half = N_BUCKETS // 2
    max_exact = half // 2
    n = jnp.abs(rel)
    side = jnp.where(rel > 0, half, 0)
    nf = jnp.maximum(n, 1).astype(jnp.float32)
    large = max_exact + (jnp.log(nf / max_exact) / math.log(MAX_DISTANCE / max_exact)
                         * (half - max_exact)).astype(jnp.int32)
    large = jnp.minimum(large, half - 1)
    return side + jnp.where(n < max_exact, n, large)


def _bias_tiles(rel_bias, g):
    r = DILATIONS[g]
    offs = r * jnp.arange(-HALF_KEYS, HALF_KEYS + 1, dtype=jnp.int32)
    cols = rel_bias[_t5_bucket(offs)][:, g * HEADS_PER_GROUP:(g + 1) * HEADS_PER_GROUP]
    cols = cols.astype(F32).T
    a = np.arange(Q_TILE)[:, None]
    c = np.arange(K_TILE)[None, :]
    tiles = []
    for v in range(3):
        j = c - a - HALF_KEYS * v
        inside = np.abs(j) <= HALF_KEYS
        idx = np.clip(j + HALF_KEYS, 0, 2 * HALF_KEYS)
        tiles.append(jnp.where(inside[None], cols[:, idx], NEG_INF))
    return jnp.stack(tiles)


def kernel(x, norm_w, conv_in, conv_w, conv_out, attn_qkv, attn_out, rel_bias,
           ffn_up, ffn_conv_w, ffn_conv_b, ffn_down, final_norm):
    d = D_MODEL
    gw = GROUP_WIDTH
    fw = final_norm.reshape(1, d)

    x = _mixer(x, norm_w[0, 0].reshape(1, d), conv_in[0].astype(BF16), conv_w[0],
               conv_out[0].astype(BF16))
    x = _ffn(x, norm_w[0, 1].reshape(1, d), ffn_up[0].astype(BF16), ffn_conv_w[0],
             ffn_conv_b[0].reshape(1, D_FF), ffn_down[0].astype(BF16), fw, False)

    w_qkv = attn_qkv[0].reshape(d, 3, N_GROUPS, gw).transpose(0, 2, 1, 3)
    w_qkv = w_qkv.reshape(d, 3 * N_GROUPS * gw).astype(BF16)
    qkvs = _qkv(x, norm_w[1, 0].reshape(1, d), w_qkv)
    outs, lses = [], []
    for g, r in enumerate(DILATIONS):
        o, lse = _attention(qkvs[g], _bias_tiles(rel_bias, g), r)
        outs.append(o)
        lses.append(lse)
    x = _combine(x, outs, lses, attn_out[0].astype(BF16))
    x = _ffn(x, norm_w[1, 1].reshape(1, d), ffn_up[1].astype(BF16), ffn_conv_w[1],
             ffn_conv_b[1].reshape(1, D_FF), ffn_down[1].astype(BF16), fw, True)
    return x
```

```python
import functools
import math

import jax
import jax.numpy as jnp
import numpy as np
from jax import lax
from jax.experimental import pallas as pl
from jax.experimental.pallas import tpu as pltpu

D_MODEL = 1024
D_FF = 2816
EPS = 1e-6
DILATIONS = (1, 4, 16)
N_GROUPS = 3
HEADS_PER_GROUP = 8
HEAD_DIM = 64
GROUP_WIDTH = HEADS_PER_GROUP * HEAD_DIM
N_BUCKETS = 32
MAX_DISTANCE = 1024
NEG_INF = -1e30

HALF_KEYS = 64
LANES = 128
HALO = 16
ROW_TILE = 512
COL_CHUNK = 256
Q_TILE = 128
K_TILE = Q_TILE + 2 * HALF_KEYS
ATTN_ROWS = 512
VMEM_LIMIT = 56 * 1024 * 1024

BF16 = jnp.bfloat16
F32 = jnp.float32


def _dot(a, b):
    return jnp.dot(a, b, preferred_element_type=F32)


def _rms(x, w):
    return x * lax.rsqrt(jnp.mean(x * x, axis=-1, keepdims=True) + EPS) * w


def _resident(shape):
    zeros = (0,) * len(shape)
    return pl.BlockSpec(shape, lambda *_: zeros, pipeline_mode=pl.Buffered(1))


def _seq_specs(seq, tile):
    per = tile // HALO
    last = seq // HALO - 1
    main = pl.BlockSpec((None, tile, D_MODEL), lambda b, i: (b, i, 0))
    prev = pl.BlockSpec((None, HALO, D_MODEL),
                        lambda b, i: (b, jnp.maximum(i * per - 1, 0), 0))
    nxt = pl.BlockSpec((None, HALO, D_MODEL),
                       lambda b, i: (b, jnp.minimum((i + 1) * per, last), 0))
    return main, prev, nxt


def _fill_normed(hn_ref, xp_ref, x_ref, xn_ref, nw):
    tile = x_ref.shape[0]
    hn_ref[0:HALO, :] = _rms(xp_ref[...], nw).astype(BF16)
    hn_ref[HALO:HALO + tile, :] = _rms(x_ref[...], nw).astype(BF16)
    hn_ref[HALO + tile:, :] = _rms(xn_ref[...], nw).astype(BF16)


def _conv3(p, cw_ref, cols, tile):
    i = pl.program_id(1)
    rows = lax.broadcasted_iota(jnp.int32, (p.shape[0], 1), 0)
    outside = ((i == 0) & (rows < HALO)) | (
        (i == pl.num_programs(1) - 1) & (rows >= HALO + tile))
    p = jnp.where(outside, 0.0, p)
    n = p.shape[0]
    before = pltpu.roll(p, 1, axis=0)[HALO:HALO + tile]
    after = pltpu.roll(p, n - 1, axis=0)[HALO:HALO + tile]
    mid = p[HALO:HALO + tile]
    return (cw_ref[0:1, cols] * before + cw_ref[1:2, cols] * mid
            + cw_ref[2:3, cols] * after)


def _mixer_kernel(x_ref, xp_ref, xn_ref, nw_ref, win_ref, cw_ref, wout_ref,
                  o_ref, hn_ref, a_ref):
    tile = x_ref.shape[0]
    d = D_MODEL
    _fill_normed(hn_ref, xp_ref, x_ref, xn_ref, nw_ref[...])
    for j in range(d // COL_CHUNK):
        cols = slice(j * COL_CHUNK, (j + 1) * COL_CHUNK)
        gate = _dot(hn_ref[HALO:HALO + tile, :], win_ref[:, cols])
        c = _dot(hn_ref[...], win_ref[:, d + j * COL_CHUNK:d + (j + 1) * COL_CHUNK])
        h = _dot(hn_ref[...], win_ref[:, 2 * d + j * COL_CHUNK:2 * d + (j + 1) * COL_CHUNK])
        conv = _conv3(c * h, cw_ref, cols, tile)
        a_ref[:, cols] = (gate * conv).astype(BF16)
    o_ref[...] = x_ref[...] + _dot(a_ref[...], wout_ref[...])


def _ffn_kernel(x_ref, xp_ref, xn_ref, nw_ref, wup_ref, cw_ref, cb_ref, wdown_ref,
                fw_ref, o_ref, hn_ref, a_ref, *, final_norm):
    tile = x_ref.shape[0]
    f = D_FF
    _fill_normed(hn_ref, xp_ref, x_ref, xn_ref, nw_ref[...])
    for j in range(f // COL_CHUNK):
        cols = slice(j * COL_CHUNK, (j + 1) * COL_CHUNK)
        g = _dot(hn_ref[...], wup_ref[:, cols])
        u = _dot(hn_ref[HALO:HALO + tile, :],
                 wup_ref[:, f + j * COL_CHUNK:f + (j + 1) * COL_CHUNK])
        g = _conv3(g, cw_ref, cols, tile) + cb_ref[:, cols]
        act = g / (1.0 + jnp.exp(-g))
        a_ref[:, cols] = (act * u).astype(BF16)
    y = x_ref[...] + _dot(a_ref[...], wdown_ref[...])
    if final_norm:
        y = _rms(y, fw_ref[...])
    o_ref[...] = y


def _qkv_kernel(x_ref, nw_ref, w_ref, o0_ref, o1_ref, o2_ref, hn_ref, hp_ref):
    tile = x_ref.shape[0]
    gw3 = 3 * GROUP_WIDTH
    hn = _rms(x_ref[...], nw_ref[...])
    for c in range(D_MODEL // LANES):
        hn_ref[c] = hn[:, c * LANES:(c + 1) * LANES]
    for g, (r, o_ref) in enumerate(zip(DILATIONS, (o0_ref, o1_ref, o2_ref))):
        rows = tile // r
        for p in range(r):
            for c in range(D_MODEL // LANES):
                hp_ref[p * rows:(p + 1) * rows, c * LANES:(c + 1) * LANES] = (
                    hn_ref[c, pl.ds(p, rows, stride=r), :].astype(BF16))
        res = _dot(hp_ref[...], w_ref[:, g * gw3:(g + 1) * gw3])
        for p in range(r):
            blk = res[p * rows:(p + 1) * rows]
            o_ref[p, :, 0:GROUP_WIDTH] = (
                blk[:, 0:GROUP_WIDTH] * (HEAD_DIM ** -0.5)).astype(BF16)
            o_ref[p, :, GROUP_WIDTH:] = blk[:, GROUP_WIDTH:].astype(BF16)


def _attn_kernel(q_ref, k_ref, v_ref, bias_ref, o_ref, m_ref, d_ref, *, phase_len):
    n_phase, n_rows, _ = q_ref.shape
    t = pl.program_id(2)
    lane = lax.broadcasted_iota(jnp.int32, (1, LANES), 1)
    low = lane < HEAD_DIM
    for ph in range(n_phase):
        for st in range(n_rows // Q_TILE):
            q0 = t * n_rows + st * Q_TILE
            start = jnp.clip(q0 - HALF_KEYS, 0, phase_len - K_TILE)
            start = pl.multiple_of(start, HALF_KEYS)
            variant = lax.shift_right_logical(q0 - start, 6)
            rows = slice(st * Q_TILE, (st + 1) * Q_TILE)
            m_all = jnp.zeros((Q_TILE, LANES), F32)
            d_all = jnp.ones((Q_TILE, LANES), F32)
            for hp in range(HEADS_PER_GROUP // 2):
                cols = slice(hp * LANES, (hp + 1) * LANES)
                q2 = q_ref[ph, rows, cols]
                k2 = k_ref[ph, pl.ds(start, K_TILE), cols]
                v2 = v_ref[ph, pl.ds(start, K_TILE), cols]
                zero = jnp.zeros_like(q2)
                qs = jnp.concatenate([jnp.where(low, q2, zero), jnp.where(low, zero, q2)], axis=0)
                s = lax.dot_general(qs, k2, (((1,), (1,)), ((), ())),
                                    preferred_element_type=F32)
                s = s + bias_ref[variant, hp]
                m = jnp.max(s, axis=-1, keepdims=True)
                e = jnp.exp(s - m)
                den = jnp.sum(e, axis=-1, keepdims=True)
                pv = _dot(e.astype(BF16), v2)
                o_ref[ph, rows, cols] = jnp.where(low, pv[:Q_TILE], pv[Q_TILE:]).astype(BF16)
                for half in range(2):
                    sel = lane == 2 * hp + half
                    part = slice(half * Q_TILE, (half + 1) * Q_TILE)
                    m_all = jnp.where(sel, m[part], m_all)
                    d_all = jnp.where(sel, den[part], d_all)
            m_ref[ph, rows, :] = m_all
            d_ref[ph, rows, :] = d_all


def _combine_kernel(x_ref, o0_ref, o1_ref, o2_ref, m0_ref, m1_ref, m2_ref,
                    d0_ref, d1_ref, d2_ref, expand_ref, w_ref,
                    out_ref, on_ref, mn_ref, dn_ref, a_ref):
    tile = x_ref.shape[0]
    nchunk = GROUP_WIDTH // LANES
    groups = zip(DILATIONS, (o0_ref, o1_ref, o2_ref), (m0_ref, m1_ref, m2_ref),
                 (d0_ref, d1_ref, d2_ref))
    for g, (r, o_ref, m_ref, d_ref) in enumerate(groups):
        rows = tile // r
        for p in range(r):
            o = o_ref[p].astype(F32)
            for c in range(nchunk):
                on_ref[g, c, pl.ds(p, rows, stride=r), :] = o[:, c * LANES:(c + 1) * LANES]
            mn_ref[g, pl.ds(p, rows, stride=r), :] = m_ref[p]
            dn_ref[g, pl.ds(p, rows, stride=r), :] = d_ref[p]
    m_max = jnp.maximum(jnp.maximum(mn_ref[0], mn_ref[1]), mn_ref[2])
    w = [jnp.exp(mn_ref[g] - m_max) for g in range(N_GROUPS)]
    den = w[0] * dn_ref[0] + w[1] * dn_ref[1] + w[2] * dn_ref[2]
    acc = [None] * nchunk
    for g in range(N_GROUPS):
        alpha = w[g] / den
        hi = alpha.astype(BF16)
        lo = (alpha - hi.astype(F32)).astype(BF16)
        wide = _dot(hi, expand_ref[...]) + _dot(lo, expand_ref[...])
        for c in range(nchunk):
            term = wide[:, c * LANES:(c + 1) * LANES] * on_ref[g, c]
            acc[c] = term if g == 0 else acc[c] + term
    for c in range(nchunk):
        a_ref[:, c * LANES:(c + 1) * LANES] = acc[c].astype(BF16)
    out_ref[...] = x_ref[...] + _dot(a_ref[...], w_ref[...])


def _params(n_axes=2):
    return pltpu.CompilerParams(
        dimension_semantics=("arbitrary",) * n_axes, vmem_limit_bytes=VMEM_LIMIT)


def _mixer(x, nw, w_in, cw, w_out):
    b, s, d = x.shape
    main, prev, nxt = _seq_specs(s, ROW_TILE)
    return pl.pallas_call(
        _mixer_kernel,
        grid=(b, s // ROW_TILE),
        in_specs=[main, prev, nxt, _resident((1, d)), _resident((d, 3 * d)),
                  _resident((3, d)), _resident((d, d))],
        out_specs=main,
        out_shape=jax.ShapeDtypeStruct(x.shape, F32),
        scratch_shapes=[pltpu.VMEM((ROW_TILE + 2 * HALO, d), BF16),
                        pltpu.VMEM((ROW_TILE, d), BF16)],
        compiler_params=_params(),
        name="short_conv_mixer",
    )(x, x, x, nw, w_in, cw, w_out)


def _ffn(x, nw, w_up, cw, cb, w_down, fw, final_norm):
    b, s, d = x.shape
    f = D_FF
    main, prev, nxt = _seq_specs(s, ROW_TILE)
    return pl.pallas_call(
        functools.partial(_ffn_kernel, final_norm=final_norm),
        grid=(b, s // ROW_TILE),
        in_specs=[main, prev, nxt, _resident((1, d)), _resident((d, 2 * f)),
                  _resident((3, f)), _resident((1, f)), _resident((f, d)),
                  _resident((1, d))],
        out_specs=main,
        out_shape=jax.ShapeDtypeStruct(x.shape, F32),
        scratch_shapes=[pltpu.VMEM((ROW_TILE + 2 * HALO, d), BF16),
                        pltpu.VMEM((ROW_TILE, f), BF16)],
        compiler_params=_params(),
        name="conv_ffn_final" if final_norm else "conv_ffn",
    )(x, x, x, nw, w_up, cw, cb, w_down, fw)


def _qkv(x, nw, w):
    b, s, d = x.shape
    gw3 = 3 * GROUP_WIDTH
    out_shapes, out_specs = [], []
    for r in DILATIONS:
        out_shapes.append(jax.ShapeDtypeStruct((b, r, s // r, gw3), BF16))
        out_specs.append(pl.BlockSpec((None, r, ROW_TILE // r, gw3),
                                      lambda bi, i: (bi, 0, i, 0)))
    return pl.pallas_call(
        _qkv_kernel,
        grid=(b, s // ROW_TILE),
        in_specs=[pl.BlockSpec((None, ROW_TILE, d), lambda bi, i: (bi, i, 0)),
                  _resident((1, d)), _resident((d, N_GROUPS * gw3))],
        out_specs=out_specs,
        out_shape=out_shapes,
        scratch_shapes=[pltpu.VMEM((d // LANES, ROW_TILE, LANES), F32),
                        pltpu.VMEM((ROW_TILE, d), BF16)],
        compiler_params=_params(),
        name="qkv_proj",
    )(x, nw, w)


def _attention(qkv, bias, r):
    b, _, phase_len, _ = qkv.shape
    gw = GROUP_WIDTH
    n_rows = min(ATTN_ROWS, phase_len)
    n_phase = ATTN_ROWS // n_rows
    q_spec = pl.BlockSpec((None, n_phase, n_rows, gw), lambda bi, p, t: (bi, p, t, 0))
    k_spec = pl.BlockSpec((None, n_phase, phase_len, gw), lambda bi, p, t: (bi, p, 0, 1))
    v_spec = pl.BlockSpec((None, n_phase, phase_len, gw), lambda bi, p, t: (bi, p, 0, 2))
    stat_spec = pl.BlockSpec((None, n_phase, n_rows, LANES), lambda bi, p, t: (bi, p, t, 0))
    stat_shape = jax.ShapeDtypeStruct((b, r, phase_len, LANES), F32)
    return pl.pallas_call(
        functools.partial(_attn_kernel, phase_len=phase_len),
        grid=(b, r // n_phase, phase_len // n_rows),
        in_specs=[q_spec, k_spec, v_spec, _resident(bias.shape)],
        out_specs=[q_spec, stat_spec, stat_spec],
        out_shape=[jax.ShapeDtypeStruct((b, r, phase_len, gw), BF16), stat_shape, stat_shape],
        compiler_params=_params(3),
        name=f"local_attn_d{r}",
    )(qkv, qkv, qkv, bias)


def _combine(x, outs, maxes, dens, w_out):
    b, s, d = x.shape
    gw = GROUP_WIDTH
    row = pl.BlockSpec((None, ROW_TILE, d), lambda bi, i: (bi, i, 0))
    o_specs = [pl.BlockSpec((None, r, ROW_TILE // r, gw), lambda bi, i: (bi, 0, i, 0))
               for r in DILATIONS]
    stat_specs = [pl.BlockSpec((None, r, ROW_TILE // r, LANES), lambda bi, i: (bi, 0, i, 0))
                  for r in DILATIONS]
    expand = np.zeros((LANES, gw), np.float32)
    for h in range(HEADS_PER_GROUP):
        expand[h, h * HEAD_DIM:(h + 1) * HEAD_DIM] = 1.0
    return pl.pallas_call(
        _combine_kernel,
        grid=(b, s // ROW_TILE),
        in_specs=[row] + o_specs + stat_specs + stat_specs
        + [_resident((LANES, gw)), _resident((gw, d))],
        out_specs=row,
        out_shape=jax.ShapeDtypeStruct(x.shape, F32),
        scratch_shapes=[pltpu.VMEM((N_GROUPS, gw // LANES, ROW_TILE, LANES), F32),
                        pltpu.VMEM((N_GROUPS, ROW_TILE, LANES), F32),
                        pltpu.VMEM((N_GROUPS, ROW_TILE, LANES), F32),
                        pltpu.VMEM((ROW_TILE, gw), BF16)],
        compiler_params=_params(),
        name="attn_combine_out",
    )(x, *outs, *maxes, *dens, jnp.asarray(expand, BF16), w_out)


def _t5_bucket(rel):
    half = N_BUCKETS // 2
    max_exact = half // 2
    n = jnp.abs(rel)
    side = jnp.where(rel > 0, half, 0)
    nf = jnp.maximum(n, 1).astype(jnp.float32)
    large = max_exact + (jnp.log(nf / max_exact) / math.log(MAX_DISTANCE / max_exact)
                         * (half - max_exact)).astype(jnp.int32)
    large = jnp.minimum(large, half - 1)
    return side + jnp.where(n < max_exact, n, large)


def _bias_tiles(rel_bias, g):
    r = DILATIONS[g]
    heads = HEADS_PER_GROUP
    offs = r * jnp.arange(-HALF_KEYS, HALF_KEYS + 1, dtype=jnp.int32)
    onehot = _t5_bucket(offs)[:, None] == jnp.arange(N_BUCKETS)[None, :]
    table = rel_bias[:, g * heads:(g + 1) * heads].astype(F32)
    per_offset = jnp.sum(jnp.where(onehot[:, :, None], table[None], 0.0), axis=1)
    n = 4 * Q_TILE
    row = jnp.full((heads, n + 1), NEG_INF, F32)
    row = row.at[:, 2 * Q_TILE - HALF_KEYS:2 * Q_TILE + HALF_KEYS + 1].set(per_offset.T)
    skew = jnp.tile(row, (1, Q_TILE))[:, :Q_TILE * n].reshape(heads, Q_TILE, n)
    tiles = jnp.stack([skew[:, :, 2 * Q_TILE - HALF_KEYS * v:2 * Q_TILE - HALF_KEYS * v + K_TILE]
                       for v in range(3)])
    return tiles.reshape(3, heads // 2, 2 * Q_TILE, K_TILE)


def kernel(x, norm_w, conv_in, conv_w, conv_out, attn_qkv, attn_out, rel_bias,
           ffn_up, ffn_conv_w, ffn_conv_b, ffn_down, final_norm):
    d = D_MODEL
    gw = GROUP_WIDTH
    fw = final_norm.reshape(1, d)

    x = _mixer(x, norm_w[0, 0].reshape(1, d), conv_in[0].astype(BF16), conv_w[0],
               conv_out[0].astype(BF16))
    x = _ffn(x, norm_w[0, 1].reshape(1, d), ffn_up[0].astype(BF16), ffn_conv_w[0],
             ffn_conv_b[0].reshape(1, D_FF), ffn_down[0].astype(BF16), fw, False)

    w_qkv = attn_qkv[0].reshape(d, 3, N_GROUPS, gw).transpose(0, 2, 1, 3)
    w_qkv = w_qkv.reshape(d, 3 * N_GROUPS * gw).astype(BF16)
    qkvs = _qkv(x, norm_w[1, 0].reshape(1, d), w_qkv)
    outs, maxes, dens = [], [], []
    for g, r in enumerate(DILATIONS):
        o, m, den = _attention(qkvs[g], _bias_tiles(rel_bias, g), r)
        outs.append(o)
        maxes.append(m)
        dens.append(den)
    x = _combine(x, outs, maxes, dens, attn_out[0].astype(BF16))
    x = _ffn(x, norm_w[1, 1].reshape(1, d), ffn_up[1].astype(BF16), ffn_conv_w[1],
             ffn_conv_b[1].reshape(1, D_FF), ffn_down[1].astype(BF16), fw, True)
    return x
```

```python
import functools
import math

import jax
import jax.numpy as jnp
import numpy as np
from jax import lax
from jax.experimental import pallas as pl
from jax.experimental.pallas import tpu as pltpu

D_MODEL = 1024
D_FF = 2816
EPS = 1e-6
DILATIONS = (1, 4, 16)
N_GROUPS = 3
HEADS_PER_GROUP = 8
HEAD_DIM = 64
GROUP_WIDTH = HEADS_PER_GROUP * HEAD_DIM
N_BUCKETS = 32
MAX_DISTANCE = 1024
NEG_INF = -1e30

HALF_KEYS = 64
LANES = 128
HALO = 16
ROW_TILE = 1024
QKV_TILE = 512
COL_CHUNK = 256
Q_TILE = 128
K_TILE = Q_TILE + 2 * HALF_KEYS
ATTN_ROWS = 1024
LOG2E = math.log2(math.e)
VMEM_LIMIT = 56 * 1024 * 1024

BF16 = jnp.bfloat16
F32 = jnp.float32


def _dot(a, b):
    return jnp.dot(a, b, preferred_element_type=F32)


def _rms(x, w):
    return x * lax.rsqrt(jnp.mean(x * x, axis=-1, keepdims=True) + EPS) * w


def _resident(shape):
    zeros = (0,) * len(shape)
    return pl.BlockSpec(shape, lambda *_: zeros, pipeline_mode=pl.Buffered(1))


def _seq_specs(seq, tile):
    per = tile // HALO
    last = seq // HALO - 1
    main = pl.BlockSpec((None, tile, D_MODEL), lambda b, i: (b, i, 0))
    prev = pl.BlockSpec((None, HALO, D_MODEL),
                        lambda b, i: (b, jnp.maximum(i * per - 1, 0), 0))
    nxt = pl.BlockSpec((None, HALO, D_MODEL),
                       lambda b, i: (b, jnp.minimum((i + 1) * per, last), 0))
    return main, prev, nxt


def _fill_normed(hn_ref, xp_ref, x_ref, xn_ref, nw):
    tile = x_ref.shape[0]
    hn_ref[0:HALO, :] = _rms(xp_ref[...], nw).astype(BF16)
    hn_ref[HALO:HALO + tile, :] = _rms(x_ref[...], nw).astype(BF16)
    hn_ref[HALO + tile:, :] = _rms(xn_ref[...], nw).astype(BF16)


def _conv3(p, cw_ref, cols, tile):
    i = pl.program_id(1)
    rows = lax.broadcasted_iota(jnp.int32, (p.shape[0], 1), 0)
    outside = ((i == 0) & (rows < HALO)) | (
        (i == pl.num_programs(1) - 1) & (rows >= HALO + tile))
    p = jnp.where(outside, 0.0, p)
    n = p.shape[0]
    before = pltpu.roll(p, 1, axis=0)[HALO:HALO + tile]
    after = pltpu.roll(p, n - 1, axis=0)[HALO:HALO + tile]
    mid = p[HALO:HALO + tile]
    return (cw_ref[0:1, cols] * before + cw_ref[1:2, cols] * mid
            + cw_ref[2:3, cols] * after)


def _mixer_kernel(x_ref, xp_ref, xn_ref, nw_ref, win_ref, cw_ref, wout_ref,
                  o_ref, hn_ref, a_ref):
    tile = x_ref.shape[0]
    d = D_MODEL
    _fill_normed(hn_ref, xp_ref, x_ref, xn_ref, nw_ref[...])
    for j in range(d // COL_CHUNK):
        cols = slice(j * COL_CHUNK, (j + 1) * COL_CHUNK)
        gate = _dot(hn_ref[HALO:HALO + tile, :], win_ref[:, cols])
        c = _dot(hn_ref[...], win_ref[:, d + j * COL_CHUNK:d + (j + 1) * COL_CHUNK])
        h = _dot(hn_ref[...], win_ref[:, 2 * d + j * COL_CHUNK:2 * d + (j + 1) * COL_CHUNK])
        conv = _conv3(c * h, cw_ref, cols, tile)
        a_ref[:, cols] = (gate * conv).astype(BF16)
    o_ref[...] = x_ref[...] + _dot(a_ref[...], wout_ref[...])


def _ffn_kernel(x_ref, xp_ref, xn_ref, nw_ref, wup_ref, cw_ref, cb_ref, wdown_ref,
                fw_ref, o_ref, hn_ref, a_ref, *, final_norm):
    tile = x_ref.shape[0]
    f = D_FF
    _fill_normed(hn_ref, xp_ref, x_ref, xn_ref, nw_ref[...])
    for j in range(f // COL_CHUNK):
        cols = slice(j * COL_CHUNK, (j + 1) * COL_CHUNK)
        g = _dot(hn_ref[...], wup_ref[:, cols])
        u = _dot(hn_ref[HALO:HALO + tile, :],
                 wup_ref[:, f + j * COL_CHUNK:f + (j + 1) * COL_CHUNK])
        g = _conv3(g, cw_ref, cols, tile) + cb_ref[:, cols]
        act = g / (1.0 + jnp.exp(-g))
        a_ref[:, cols] = (act * u).astype(BF16)
    y = x_ref[...] + _dot(a_ref[...], wdown_ref[...])
    if final_norm:
        y = _rms(y, fw_ref[...])
    o_ref[...] = y


def _qkv_kernel(x_ref, nw_ref, w_ref, o0_ref, o1_ref, o2_ref, hn_ref, hp_ref):
    tile = x_ref.shape[0]
    gw3 = 3 * GROUP_WIDTH
    hn = _rms(x_ref[...], nw_ref[...])
    for c in range(D_MODEL // LANES):
        hn_ref[c] = hn[:, c * LANES:(c + 1) * LANES]
    for g, (r, o_ref) in enumerate(zip(DILATIONS, (o0_ref, o1_ref, o2_ref))):
        rows = tile // r
        for p in range(r):
            for c in range(D_MODEL // LANES):
                hp_ref[p * rows:(p + 1) * rows, c * LANES:(c + 1) * LANES] = (
                    hn_ref[c, pl.ds(p, rows, stride=r), :].astype(BF16))
        res = _dot(hp_ref[...], w_ref[:, g * gw3:(g + 1) * gw3])
        for p in range(r):
            blk = res[p * rows:(p + 1) * rows]
            o_ref[p, :, 0:GROUP_WIDTH] = (
                blk[:, 0:GROUP_WIDTH] * (HEAD_DIM ** -0.5 * LOG2E)).astype(BF16)
            o_ref[p, :, GROUP_WIDTH:] = blk[:, GROUP_WIDTH:].astype(BF16)


def _attn_kernel(q_ref, k_ref, v_ref, bias_ref, o_ref, m_ref, d_ref, *, phase_len):
    n_phase, n_rows, _ = q_ref.shape
    t = pl.program_id(2)
    lane = lax.broadcasted_iota(jnp.int32, (1, LANES), 1)
    low = lane < HEAD_DIM
    for ph in range(n_phase):
        for st in range(n_rows // Q_TILE):
            q0 = t * n_rows + st * Q_TILE
            start = jnp.clip(q0 - HALF_KEYS, 0, phase_len - K_TILE)
            start = pl.multiple_of(start, HALF_KEYS)
            variant = lax.shift_right_logical(q0 - start, 6)
            rows = slice(st * Q_TILE, (st + 1) * Q_TILE)
            m_all = jnp.zeros((Q_TILE, LANES), F32)
            d_all = jnp.ones((Q_TILE, LANES), F32)
            for hp in range(HEADS_PER_GROUP // 2):
                cols = slice(hp * LANES, (hp + 1) * LANES)
                q2 = q_ref[ph, rows, cols]
                k2 = k_ref[ph, pl.ds(start, K_TILE), cols]
                v2 = v_ref[ph, pl.ds(start, K_TILE), cols]
                zero = jnp.zeros_like(q2)
                qs = jnp.concatenate([jnp.where(low, q2, zero), jnp.where(low, zero, q2)], axis=0)
                s = lax.dot_general(qs, k2, (((1,), (1,)), ((), ())),
                                    preferred_element_type=F32)
                s = s + bias_ref[variant, hp]
                m = jnp.max(s, axis=-1, keepdims=True)
                e = jnp.exp2(s - m).astype(BF16)
                pv = _dot(e, jnp.concatenate([v2, jnp.ones_like(v2)], axis=1))
                num = pv[:, :LANES]
                den = pv[:, LANES:]
                o_ref[ph, rows, cols] = jnp.where(low, num[:Q_TILE], num[Q_TILE:]).astype(BF16)
                for half in range(2):
                    sel = lane == 2 * hp + half
                    part = slice(half * Q_TILE, (half + 1) * Q_TILE)
                    m_all = jnp.where(sel, m[part], m_all)
                    d_all = jnp.where(sel, den[part], d_all)
            m_ref[ph, rows, :] = m_all
            d_ref[ph, rows, :] = d_all


def _combine_kernel(x_ref, o0_ref, o1_ref, o2_ref, m0_ref, m1_ref, m2_ref,
                    d0_ref, d1_ref, d2_ref, expand_ref, w_ref,
                    out_ref, on_ref, mn_ref, dn_ref, a_ref):
    tile = x_ref.shape[0]
    nchunk = GROUP_WIDTH // LANES
    groups = zip(DILATIONS, (o0_ref, o1_ref, o2_ref), (m0_ref, m1_ref, m2_ref),
                 (d0_ref, d1_ref, d2_ref))
    for g, (r, o_ref, m_ref, d_ref) in enumerate(groups):
        rows = tile // r
        for p in range(r):
            o = o_ref[p].astype(F32)
            for c in range(nchunk):
                on_ref[g, c, pl.ds(p, rows, stride=r), :] = o[:, c * LANES:(c + 1) * LANES]
            mn_ref[g, pl.ds(p, rows, stride=r), :] = m_ref[p]
            dn_ref[g, pl.ds(p, rows, stride=r), :] = d_ref[p]
    m_max = jnp.maximum(jnp.maximum(mn_ref[0], mn_ref[1]), mn_ref[2])
    w = [jnp.exp2(mn_ref[g] - m_max) for g in range(N_GROUPS)]
    den = w[0] * dn_ref[0] + w[1] * dn_ref[1] + w[2] * dn_ref[2]
    acc = [None] * nchunk
    for g in range(N_GROUPS):
        alpha = w[g] / den
        hi = alpha.astype(BF16)
        lo = (alpha - hi.astype(F32)).astype(BF16)
        wide = _dot(hi, expand_ref[...]) + _dot(lo, expand_ref[...])
        for c in range(nchunk):
            term = wide[:, c * LANES:(c + 1) * LANES] * on_ref[g, c]
            acc[c] = term if g == 0 else acc[c] + term
    for c in range(nchunk):
        a_ref[:, c * LANES:(c + 1) * LANES] = acc[c].astype(BF16)
    out_ref[...] = x_ref[...] + _dot(a_ref[...], w_ref[...])


def _params(n_axes=2):
    return pltpu.CompilerParams(
        dimension_semantics=("arbitrary",) * n_axes, vmem_limit_bytes=VMEM_LIMIT)


def _mixer(x, nw, w_in, cw, w_out):
    b, s, d = x.shape
    main, prev, nxt = _seq_specs(s, ROW_TILE)
    return pl.pallas_call(
        _mixer_kernel,
        grid=(b, s // ROW_TILE),
        in_specs=[main, prev, nxt, _resident((1, d)), _resident((d, 3 * d)),
                  _resident((3, d)), _resident((d, d))],
        out_specs=main,
        out_shape=jax.ShapeDtypeStruct(x.shape, F32),
        scratch_shapes=[pltpu.VMEM((ROW_TILE + 2 * HALO, d), BF16),
                        pltpu.VMEM((ROW_TILE, d), BF16)],
        compiler_params=_params(),
        name="short_conv_mixer",
    )(x, x, x, nw, w_in, cw, w_out)


def _ffn(x, nw, w_up, cw, cb, w_down, fw, final_norm):
    b, s, d = x.shape
    f = D_FF
    main, prev, nxt = _seq_specs(s, ROW_TILE)
    return pl.pallas_call(
        functools.partial(_ffn_kernel, final_norm=final_norm),
        grid=(b, s // ROW_TILE),
        in_specs=[main, prev, nxt, _resident((1, d)), _resident((d, 2 * f)),
                  _resident((3, f)), _resident((1, f)), _resident((f, d)),
                  _resident((1, d))],
        out_specs=main,
        out_shape=jax.ShapeDtypeStruct(x.shape, F32),
        scratch_shapes=[pltpu.VMEM((ROW_TILE + 2 * HALO, d), BF16),
                        pltpu.VMEM((ROW_TILE, f), BF16)],
        compiler_params=_params(),
        name="conv_ffn_final" if final_norm else "conv_ffn",
    )(x, x, x, nw, w_up, cw, cb, w_down, fw)


def _qkv(x, nw, w):
    b, s, d = x.shape
    gw3 = 3 * GROUP_WIDTH
    out_shapes, out_specs = [], []
    for r in DILATIONS:
        out_shapes.append(jax.ShapeDtypeStruct((b, r, s // r, gw3), BF16))
        out_specs.append(pl.BlockSpec((None, r, QKV_TILE // r, gw3),
                                      lambda bi, i: (bi, 0, i, 0)))
    return pl.pallas_call(
        _qkv_kernel,
        grid=(b, s // QKV_TILE),
        in_specs=[pl.BlockSpec((None, QKV_TILE, d), lambda bi, i: (bi, i, 0)),
                  _resident((1, d)), _resident((d, N_GROUPS * gw3))],
        out_specs=out_specs,
        out_shape=out_shapes,
        scratch_shapes=[pltpu.VMEM((d // LANES, QKV_TILE, LANES), F32),
                        pltpu.VMEM((QKV_TILE, d), BF16)],
        compiler_params=_params(),
        name="qkv_proj",
    )(x, nw, w)


def _attention(qkv, bias, r):
    b, _, phase_len, _ = qkv.shape
    gw = GROUP_WIDTH
    n_rows = min(ATTN_ROWS, phase_len)
    n_phase = ATTN_ROWS // n_rows
    q_spec = pl.BlockSpec((None, n_phase, n_rows, gw), lambda bi, p, t: (bi, p, t, 0))
    k_spec = pl.BlockSpec((None, n_phase, phase_len, gw), lambda bi, p, t: (bi, p, 0, 1))
    v_spec = pl.BlockSpec((None, n_phase, phase_len, gw), lambda bi, p, t: (bi, p, 0, 2))
    stat_spec = pl.BlockSpec((None, n_phase, n_rows, LANES), lambda bi, p, t: (bi, p, t, 0))
    stat_shape = jax.ShapeDtypeStruct((b, r, phase_len, LANES), F32)
    return pl.pallas_call(
        functools.partial(_attn_kernel, phase_len=phase_len),
        grid=(b, r // n_phase, phase_len // n_rows),
        in_specs=[q_spec, k_spec, v_spec, _resident(bias.shape)],
        out_specs=[q_spec, stat_spec, stat_spec],
        out_shape=[jax.ShapeDtypeStruct((b, r, phase_len, gw), BF16), stat_shape, stat_shape],
        compiler_params=_params(3),
        name=f"local_attn_d{r}",
    )(qkv, qkv, qkv, bias)


def _combine(x, outs, maxes, dens, w_out):
    b, s, d = x.shape
    gw = GROUP_WIDTH
    row = pl.BlockSpec((None, ROW_TILE, d), lambda bi, i: (bi, i, 0))
    o_specs = [pl.BlockSpec((None, r, ROW_TILE // r, gw), lambda bi, i: (bi, 0, i, 0))
               for r in DILATIONS]
    stat_specs = [pl.BlockSpec((None, r, ROW_TILE // r, LANES), lambda bi, i: (bi, 0, i, 0))
                  for r in DILATIONS]
    expand = np.zeros((LANES, gw), np.float32)
    for h in range(HEADS_PER_GROUP):
        expand[h, h * HEAD_DIM:(h + 1) * HEAD_DIM] = 1.0
    return pl.pallas_call(
        _combine_kernel,
        grid=(b, s // ROW_TILE),
        in_specs=[row] + o_specs + stat_specs + stat_specs
        + [_resident((LANES, gw)), _resident((gw, d))],
        out_specs=row,
        out_shape=jax.ShapeDtypeStruct(x.shape, F32),
        scratch_shapes=[pltpu.VMEM((N_GROUPS, gw // LANES, ROW_TILE, LANES), F32),
                        pltpu.VMEM((N_GROUPS, ROW_TILE, LANES), F32),
                        pltpu.VMEM((N_GROUPS, ROW_TILE, LANES), F32),
                        pltpu.VMEM((ROW_TILE, gw), BF16)],
        compiler_params=_params(),
        name="attn_combine_out",
    )(x, *outs, *maxes, *dens, jnp.asarray(expand, BF16), w_out)


def _t5_bucket(rel):
    half = N_BUCKETS // 2
    max_exact = half // 2
    n = jnp.abs(rel)
    side = jnp.where(rel > 0, half, 0)
    nf = jnp.maximum(n, 1).astype(jnp.float32)
    large = max_exact + (jnp.log(nf / max_exact) / math.log(MAX_DISTANCE / max_exact)
                         * (half - max_exact)).astype(jnp.int32)
    large = jnp.minimum(large, half - 1)
    return side + jnp.where(n < max_exact, n, large)


def _bias_tiles(rel_bias, g):
    r = DILATIONS[g]
    heads = HEADS_PER_GROUP
    offs = r * jnp.arange(-HALF_KEYS, HALF_KEYS + 1, dtype=jnp.int32)
    onehot = _t5_bucket(offs)[:, None] == jnp.arange(N_BUCKETS)[None, :]
    table = rel_bias[:, g * heads:(g + 1) * heads].astype(F32)
    per_offset = jnp.sum(jnp.where(onehot[:, :, None], table[None], 0.0), axis=1) * LOG2E
    n = 4 * Q_TILE
    row = jnp.full((heads, n + 1), NEG_INF, F32)
    row = row.at[:, 2 * Q_TILE - HALF_KEYS:2 * Q_TILE + HALF_KEYS + 1].set(per_offset.T)
    skew = jnp.tile(row, (1, Q_TILE))[:, :Q_TILE * n].reshape(heads, Q_TILE, n)
    tiles = jnp.stack([skew[:, :, 2 * Q_TILE - HALF_KEYS * v:2 * Q_TILE - HALF_KEYS * v + K_TILE]
                       for v in range(3)])
    return tiles.reshape(3, heads // 2, 2 * Q_TILE, K_TILE)


def kernel(x, norm_w, conv_in, conv_w, conv_out, attn_qkv, attn_out, rel_bias,
           ffn_up, ffn_conv_w, ffn_conv_b, ffn_down, final_norm):
    d = D_MODEL
    gw = GROUP_WIDTH
    fw = final_norm.reshape(1, d)

    x = _mixer(x, norm_w[0, 0].reshape(1, d), conv_in[0].astype(BF16), conv_w[0],
               conv_out[0].astype(BF16))
    x = _ffn(x, norm_w[0, 1].reshape(1, d), ffn_up[0].astype(BF16), ffn_conv_w[0],
             ffn_conv_b[0].reshape(1, D_FF), ffn_down[0].astype(BF16), fw, False)

    w_qkv = attn_qkv[0].reshape(d, 3, N_GROUPS, gw).transpose(0, 2, 1, 3)
    w_qkv = w_qkv.reshape(d, 3 * N_GROUPS * gw).astype(BF16)
    qkvs = _qkv(x, norm_w[1, 0].reshape(1, d), w_qkv)
    outs, maxes, dens = [], [], []
    for g, r in enumerate(DILATIONS):
        o, m, den = _attention(qkvs[g], _bias_tiles(rel_bias, g), r)
        outs.append(o)
        maxes.append(m)
        dens.append(den)
    x = _combine(x, outs, maxes, dens, attn_out[0].astype(BF16))
    x = _ffn(x, norm_w[1, 1].reshape(1, d), ffn_up[1].astype(BF16), ffn_conv_w[1],
             ffn_conv_b[1].reshape(1, D_FF), ffn_down[1].astype(BF16), fw, True)
    return x
```

```python
import functools
import math

import jax
import jax.numpy as jnp
import numpy as np
from jax import lax
from jax.experimental import pallas as pl
from jax.experimental.pallas import tpu as pltpu

D_MODEL = 1024
D_FF = 2816
EPS = 1e-6
DILATIONS = (1, 4, 16)
N_GROUPS = 3
HEADS_PER_GROUP = 8
HEAD_DIM = 64
GROUP_WIDTH = HEADS_PER_GROUP * HEAD_DIM
N_BUCKETS = 32
MAX_DISTANCE = 1024
NEG_INF = -1e30

HALF_KEYS = 64
LANES = 128
HALO = 16
ROW_TILE = 1024
QKV_TILE = 512
COL_CHUNK = 256
Q_TILE = 128
K_TILE = Q_TILE + 2 * HALF_KEYS
ATTN_ROWS = 1024
LOG2E = math.log2(math.e)
VMEM_LIMIT = 56 * 1024 * 1024

BF16 = jnp.bfloat16
F32 = jnp.float32


def _dot(a, b):
    return jnp.dot(a, b, preferred_element_type=F32)


def _rms(x, w):
    return x * lax.rsqrt(jnp.mean(x * x, axis=-1, keepdims=True) + EPS) * w


def _resident(shape, index=None):
    index = (0,) * len(shape) if index is None else index
    return pl.BlockSpec(shape, lambda *_: index, pipeline_mode=pl.Buffered(1))


def _seq_specs(seq, tile):
    per = tile // HALO
    last = seq // HALO - 1
    main = pl.BlockSpec((None, tile, D_MODEL), lambda b, i: (b, i, 0))
    prev = pl.BlockSpec((None, HALO, D_MODEL),
                        lambda b, i: (b, jnp.maximum(i * per - 1, 0), 0))
    nxt = pl.BlockSpec((None, HALO, D_MODEL),
                       lambda b, i: (b, jnp.minimum((i + 1) * per, last), 0))
    return main, prev, nxt


def _conv3(p, cw_ref, cols, tile):
    i = pl.program_id(1)
    rows = lax.broadcasted_iota(jnp.int32, (p.shape[0], 1), 0)
    outside = ((i == 0) & (rows < HALO)) | (
        (i == pl.num_programs(1) - 1) & (rows >= HALO + tile))
    p = jnp.where(outside, 0.0, p)
    n = p.shape[0]
    before = pltpu.roll(p, 1, axis=0)[HALO:HALO + tile]
    after = pltpu.roll(p, n - 1, axis=0)[HALO:HALO + tile]
    mid = p[HALO:HALO + tile]
    return (cw_ref[0:1, cols] * before + cw_ref[1:2, cols] * mid
            + cw_ref[2:3, cols] * after)


def _mixer_kernel(x_ref, xp_ref, xn_ref, nw_ref, win_ref, cw_ref, wout_ref,
                  o_ref, hn_ref, a_ref):
    tile = x_ref.shape[0]
    d = D_MODEL
    nw = nw_ref[...]
    hn_ref[0:HALO, :] = _rms(xp_ref[...], nw).astype(BF16)
    hn_ref[HALO:HALO + tile, :] = _rms(x_ref[...], nw).astype(BF16)
    hn_ref[HALO + tile:, :] = _rms(xn_ref[...], nw).astype(BF16)
    for j in range(d // COL_CHUNK):
        cols = slice(j * COL_CHUNK, (j + 1) * COL_CHUNK)
        gate = _dot(hn_ref[HALO:HALO + tile, :], win_ref[:, cols])
        c = _dot(hn_ref[...], win_ref[:, d + j * COL_CHUNK:d + (j + 1) * COL_CHUNK])
        h = _dot(hn_ref[...], win_ref[:, 2 * d + j * COL_CHUNK:2 * d + (j + 1) * COL_CHUNK])
        conv = _conv3(c * h, cw_ref, cols, tile)
        a_ref[:, cols] = (gate * conv).astype(BF16)
    o_ref[...] = x_ref[...] + _dot(a_ref[...], wout_ref[...])


def _ffn_kernel(*refs, has_branch, final_norm):
    if has_branch:
        x_ref, xp_ref, xn_ref, y_ref, yp_ref, yn_ref = refs[:6]
        refs = refs[6:]
        tiles = [xp_ref[...] + yp_ref[...], x_ref[...] + y_ref[...], xn_ref[...] + yn_ref[...]]
    else:
        x_ref, xp_ref, xn_ref = refs[:3]
        refs = refs[3:]
        tiles = [xp_ref[...], x_ref[...], xn_ref[...]]
    nw_ref, wup_ref, cw_ref, cb_ref, wdown_ref, fw_ref, o_ref, hn_ref, a_ref = refs
    tile = x_ref.shape[0]
    f = D_FF
    nw = nw_ref[...]
    hn_ref[0:HALO, :] = _rms(tiles[0], nw).astype(BF16)
    hn_ref[HALO:HALO + tile, :] = _rms(tiles[1], nw).astype(BF16)
    hn_ref[HALO + tile:, :] = _rms(tiles[2], nw).astype(BF16)
    for j in range(f // COL_CHUNK):
        cols = slice(j * COL_CHUNK, (j + 1) * COL_CHUNK)
        g = _dot(hn_ref[...], wup_ref[:, cols])
        u = _dot(hn_ref[HALO:HALO + tile, :],
                 wup_ref[:, f + j * COL_CHUNK:f + (j + 1) * COL_CHUNK])
        g = _conv3(g, cw_ref, cols, tile) + cb_ref[:, cols]
        act = g / (1.0 + jnp.exp(-g))
        a_ref[:, cols] = (act * u).astype(BF16)
    y = x_ref[...] + _dot(a_ref[...], wdown_ref[...])
    if has_branch:
        y = y + y_ref[...]
    if final_norm:
        y = _rms(y, fw_ref[...])
    o_ref[...] = y


def _qkv_kernel(x_ref, nw_ref, w_ref, o0_ref, o1_ref, o2_ref, hn_ref, hp_ref):
    tile = x_ref.shape[0]
    gw = GROUP_WIDTH
    hn = _rms(x_ref[...], nw_ref[...])
    for c in range(D_MODEL // LANES):
        hn_ref[c] = hn[:, c * LANES:(c + 1) * LANES]
    for g, (r, o_ref) in enumerate(zip(DILATIONS, (o0_ref, o1_ref, o2_ref))):
        rows = tile // r
        for p in range(r):
            for c in range(D_MODEL // LANES):
                hp_ref[g, p * rows:(p + 1) * rows, c * LANES:(c + 1) * LANES] = (
                    hn_ref[c, pl.ds(p, rows, stride=r), :].astype(BF16))
        for which in range(3):
            col = (which * N_GROUPS + g) * gw
            res = _dot(hp_ref[g], w_ref[:, col:col + gw])
            if which == 0:
                res = res * (HEAD_DIM ** -0.5 * LOG2E)
            for p in range(r):
                o_ref[p, :, which * gw:(which + 1) * gw] = (
                    res[p * rows:(p + 1) * rows].astype(BF16))


def _attn_kernel(q_ref, k_ref, v_ref, bias_ref, o_ref, m_ref, d_ref, *, phase_len):
    n_phase, n_rows, _ = q_ref.shape
    t = pl.program_id(2)
    lane = lax.broadcasted_iota(jnp.int32, (1, LANES), 1)
    low = lane < HEAD_DIM
    for ph in range(n_phase):
        for st in range(n_rows // Q_TILE):
            q0 = t * n_rows + st * Q_TILE
            start = jnp.clip(q0 - HALF_KEYS, 0, phase_len - K_TILE)
            start = pl.multiple_of(start, HALF_KEYS)
            variant = lax.shift_right_logical(q0 - start, 6)
            rows = slice(st * Q_TILE, (st + 1) * Q_TILE)
            m_all = jnp.zeros((Q_TILE, LANES), F32)
            d_all = jnp.ones((Q_TILE, LANES), F32)
            for hp in range(HEADS_PER_GROUP // 2):
                cols = slice(hp * LANES, (hp + 1) * LANES)
                q2 = q_ref[ph, rows, cols]
                k2 = k_ref[ph, pl.ds(start, K_TILE), cols]
                v2 = v_ref[ph, pl.ds(start, K_TILE), cols]
                zero = jnp.zeros_like(q2)
                qs = jnp.concatenate([jnp.where(low, q2, zero), jnp.where(low, zero, q2)], axis=0)
                s = lax.dot_general(qs, k2, (((1,), (1,)), ((), ())),
                                    preferred_element_type=F32)
                s = s + bias_ref[variant, hp]
                m = jnp.max(s, axis=-1, keepdims=True)
                e = jnp.exp2(s - m).astype(BF16)
                pv = _dot(e, jnp.concatenate([v2, jnp.ones_like(v2)], axis=1))
                num = pv[:, :LANES]
                den = pv[:, LANES:]
                o_ref[ph, rows, cols] = jnp.where(low, num[:Q_TILE], num[Q_TILE:]).astype(BF16)
                for half in range(2):
                    sel = lane == 2 * hp + half
                    part = slice(half * Q_TILE, (half + 1) * Q_TILE)
                    m_all = jnp.where(sel, m[part], m_all)
                    d_all = jnp.where(sel, den[part], d_all)
            m_ref[ph, rows, :] = m_all
            d_ref[ph, rows, :] = d_all


def _combine_kernel(o0_ref, o1_ref, o2_ref, m0_ref, m1_ref, m2_ref,
                    d0_ref, d1_ref, d2_ref, expand_ref, w_ref,
                    out_ref, on_ref, mn_ref, dn_ref, a_ref):
    tile = out_ref.shape[0]
    nchunk = GROUP_WIDTH // LANES
    groups = zip(DILATIONS, (o0_ref, o1_ref, o2_ref), (m0_ref, m1_ref, m2_ref),
                 (d0_ref, d1_ref, d2_ref))
    nums, maxes, dens = [], [], []
    for g, (r, o_ref, m_ref, d_ref) in enumerate(groups):
        if r == 1:
            nums.append([o_ref[0, :, c * LANES:(c + 1) * LANES].astype(F32)
                         for c in range(nchunk)])
            maxes.append(m_ref[0])
            dens.append(d_ref[0])
            continue
        rows = tile // r
        for p in range(r):
            o = o_ref[p].astype(F32)
            for c in range(nchunk):
                on_ref[g - 1, c, pl.ds(p, rows, stride=r), :] = o[:, c * LANES:(c + 1) * LANES]
            mn_ref[g - 1, pl.ds(p, rows, stride=r), :] = m_ref[p]
            dn_ref[g - 1, pl.ds(p, rows, stride=r), :] = d_ref[p]
        nums.append([on_ref[g - 1, c] for c in range(nchunk)])
        maxes.append(mn_ref[g - 1])
        dens.append(dn_ref[g - 1])
    m_max = jnp.maximum(jnp.maximum(maxes[0], maxes[1]), maxes[2])
    w = [jnp.exp2(m - m_max) for m in maxes]
    den = w[0] * dens[0] + w[1] * dens[1] + w[2] * dens[2]
    acc = [None] * nchunk
    for g in range(N_GROUPS):
        alpha = w[g] / den
        hi = alpha.astype(BF16)
        lo = (alpha - hi.astype(F32)).astype(BF16)
        wide = _dot(jnp.concatenate([hi, lo], axis=1), expand_ref[...])
        for c in range(nchunk):
            term = wide[:, c * LANES:(c + 1) * LANES] * nums[g][c]
            acc[c] = term if g == 0 else acc[c] + term
    for c in range(nchunk):
        a_ref[:, c * LANES:(c + 1) * LANES] = acc[c].astype(BF16)
    out_ref[...] = _dot(a_ref[...], w_ref[...])


def _params(n_axes=2):
    return pltpu.CompilerParams(
        dimension_semantics=("arbitrary",) * n_axes, vmem_limit_bytes=VMEM_LIMIT)


def _mixer(x, nw, w_in, cw, w_out):
    b, s, d = x.shape
    main, prev, nxt = _seq_specs(s, ROW_TILE)
    return pl.pallas_call(
        _mixer_kernel,
        grid=(b, s // ROW_TILE),
        in_specs=[main, prev, nxt, _resident((1, d)), _resident((d, 3 * d)),
                  _resident((3, d)), _resident((d, d))],
        out_specs=main,
        out_shape=jax.ShapeDtypeStruct(x.shape, F32),
        scratch_shapes=[pltpu.VMEM((ROW_TILE + 2 * HALO, d), BF16),
                        pltpu.VMEM((ROW_TILE, d), BF16)],
        compiler_params=_params(),
        name="short_conv_mixer",
    )(x, x, x, nw, w_in, cw, w_out)


def _ffn(x, branch, layer, nw, w_up, cw, cb, w_down, fw, final_norm):
    b, s, d = x.shape
    f = D_FF
    seq = list(_seq_specs(s, ROW_TILE))
    has_branch = branch is not None
    acts = [x, x, x] + ([branch, branch, branch] if has_branch else [])
    return pl.pallas_call(
        functools.partial(_ffn_kernel, has_branch=has_branch, final_norm=final_norm),
        grid=(b, s // ROW_TILE),
        in_specs=seq * (2 if has_branch else 1) + [
            _resident((1, d)), _resident((None, d, 2 * f), (layer, 0, 0)),
            _resident((3, f)), _resident((1, f)), _resident((None, f, d), (layer, 0, 0)),
            _resident((1, d))],
        out_specs=seq[0],
        out_shape=jax.ShapeDtypeStruct(x.shape, F32),
        scratch_shapes=[pltpu.VMEM((ROW_TILE + 2 * HALO, d), BF16),
                        pltpu.VMEM((ROW_TILE, f), BF16)],
        compiler_params=_params(),
        name="conv_ffn_final" if final_norm else "conv_ffn",
    )(*acts, nw, w_up, cw, cb, w_down, fw)


def _qkv(x, nw, w):
    b, s, d = x.shape
    gw3 = 3 * GROUP_WIDTH
    out_shapes, out_specs = [], []
    for r in DILATIONS:
        out_shapes.append(jax.ShapeDtypeStruct((b, r, s // r, gw3), BF16))
        out_specs.append(pl.BlockSpec((None, r, QKV_TILE // r, gw3),
                                      lambda bi, i: (bi, 0, i, 0)))
    return pl.pallas_call(
        _qkv_kernel,
        grid=(b, s // QKV_TILE),
        in_specs=[pl.BlockSpec((None, QKV_TILE, d), lambda bi, i: (bi, i, 0)),
                  _resident((1, d)), _resident((d, N_GROUPS * gw3))],
        out_specs=out_specs,
        out_shape=out_shapes,
        scratch_shapes=[pltpu.VMEM((d // LANES, QKV_TILE, LANES), F32),
                        pltpu.VMEM((N_GROUPS, QKV_TILE, d), BF16)],
        compiler_params=_params(),
        name="qkv_proj",
    )(x, nw, w)


def _attention(qkv, bias, r):
    b, _, phase_len, _ = qkv.shape
    gw = GROUP_WIDTH
    n_rows = min(ATTN_ROWS, phase_len)
    n_phase = ATTN_ROWS // n_rows
    q_spec = pl.BlockSpec((None, n_phase, n_rows, gw), lambda bi, p, t: (bi, p, t, 0))
    k_spec = pl.BlockSpec((None, n_phase, phase_len, gw), lambda bi, p, t: (bi, p, 0, 1))
    v_spec = pl.BlockSpec((None, n_phase, phase_len, gw), lambda bi, p, t: (bi, p, 0, 2))
    stat_spec = pl.BlockSpec((None, n_phase, n_rows, LANES), lambda bi, p, t: (bi, p, t, 0))
    stat_shape = jax.ShapeDtypeStruct((b, r, phase_len, LANES), F32)
    return pl.pallas_call(
        functools.partial(_attn_kernel, phase_len=phase_len),
        grid=(b, r // n_phase, phase_len // n_rows),
        in_specs=[q_spec, k_spec, v_spec, _resident(bias.shape)],
        out_specs=[q_spec, stat_spec, stat_spec],
        out_shape=[jax.ShapeDtypeStruct((b, r, phase_len, gw), BF16), stat_shape, stat_shape],
        compiler_params=_params(3),
        name=f"local_attn_d{r}",
    )(qkv, qkv, qkv, bias)


def _combine(outs, maxes, dens, w_out):
    b = outs[0].shape[0]
    s = outs[0].shape[1] * outs[0].shape[2]
    d = D_MODEL
    gw = GROUP_WIDTH
    o_specs = [pl.BlockSpec((None, r, ROW_TILE // r, gw), lambda bi, i: (bi, 0, i, 0))
               for r in DILATIONS]
    stat_specs = [pl.BlockSpec((None, r, ROW_TILE // r, LANES), lambda bi, i: (bi, 0, i, 0))
                  for r in DILATIONS]
    expand = np.zeros((2 * LANES, gw), np.float32)
    for h in range(HEADS_PER_GROUP):
        expand[h, h * HEAD_DIM:(h + 1) * HEAD_DIM] = 1.0
        expand[LANES + h, h * HEAD_DIM:(h + 1) * HEAD_DIM] = 1.0
    return pl.pallas_call(
        _combine_kernel,
        grid=(b, s // ROW_TILE),
        in_specs=o_specs + stat_specs + stat_specs
        + [_resident((2 * LANES, gw)), _resident((gw, d))],
        out_specs=pl.BlockSpec((None, ROW_TILE, d), lambda bi, i: (bi, i, 0)),
        out_shape=jax.ShapeDtypeStruct((b, s, d), F32),
        scratch_shapes=[pltpu.VMEM((N_GROUPS - 1, gw // LANES, ROW_TILE, LANES), F32),
                        pltpu.VMEM((N_GROUPS - 1, ROW_TILE, LANES), F32),
                        pltpu.VMEM((N_GROUPS - 1, ROW_TILE, LANES), F32),
                        pltpu.VMEM((ROW_TILE, gw), BF16)],
        compiler_params=_params(),
        name="attn_combine_out",
    )(*outs, *maxes, *dens, jnp.asarray(expand, BF16), w_out)


def _t5_bucket(rel):
    half = N_BUCKETS // 2
    max_exact = half // 2
    n = jnp.abs(rel)
    side = jnp.where(rel > 0, half, 0)
    nf = jnp.maximum(n, 1).astype(jnp.float32)
    large = max_exact + (jnp.log(nf / max_exact) / math.log(MAX_DISTANCE / max_exact)
                         * (half - max_exact)).astype(jnp.int32)
    large = jnp.minimum(large, half - 1)
    return side + jnp.where(n < max_exact, n, large)


def _bias_tiles(rel_bias, g):
    r = DILATIONS[g]
    heads = HEADS_PER_GROUP
    offs = r * jnp.arange(-HALF_KEYS, HALF_KEYS + 1, dtype=jnp.int32)
    onehot = _t5_bucket(offs)[:, None] == jnp.arange(N_BUCKETS)[None, :]
    table = rel_bias[:, g * heads:(g + 1) * heads].astype(F32)
    per_offset = jnp.sum(jnp.where(onehot[:, :, None], table[None], 0.0), axis=1) * LOG2E
    n = 4 * Q_TILE
    row = jnp.full((heads, n + 1), NEG_INF, F32)
    row = row.at[:, 2 * Q_TILE - HALF_KEYS:2 * Q_TILE + HALF_KEYS + 1].set(per_offset.T)
    skew = jnp.tile(row, (1, Q_TILE))[:, :Q_TILE * n].reshape(heads, Q_TILE, n)
    tiles = jnp.stack([skew[:, :, 2 * Q_TILE - HALF_KEYS * v:2 * Q_TILE - HALF_KEYS * v + K_TILE]
                       for v in range(3)])
    return tiles.reshape(3, heads // 2, 2 * Q_TILE, K_TILE)


def kernel(x, norm_w, conv_in, conv_w, conv_out, attn_qkv, attn_out, rel_bias,
           ffn_up, ffn_conv_w, ffn_conv_b, ffn_down, final_norm):
    d = D_MODEL
    fw = final_norm.reshape(1, d)
    w_up = ffn_up.astype(BF16)
    w_down = ffn_down.astype(BF16)

    x = _mixer(x, norm_w[0, 0].reshape(1, d), conv_in[0].astype(BF16), conv_w[0],
               conv_out[0].astype(BF16))
    x = _ffn(x, None, 0, norm_w[0, 1].reshape(1, d), w_up, ffn_conv_w[0],
             ffn_conv_b[0].reshape(1, D_FF), w_down, fw, False)

    qkvs = _qkv(x, norm_w[1, 0].reshape(1, d), attn_qkv[0].astype(BF16))
    outs, maxes, dens = [], [], []
    for g, r in enumerate(DILATIONS):
        o, m, den = _attention(qkvs[g], _bias_tiles(rel_bias, g), r)
        outs.append(o)
        maxes.append(m)
        dens.append(den)
    branch = _combine(outs, maxes, dens, attn_out[0].astype(BF16))
    x = _ffn(x, branch, 1, norm_w[1, 1].reshape(1, d), w_up, ffn_conv_w[1],
             ffn_conv_b[1].reshape(1, D_FF), w_down, fw, True)
    return x
```

```python
import functools
import math

import jax
import jax.numpy as jnp
import numpy as np
from jax import lax
from jax.experimental import pallas as pl
from jax.experimental.pallas import tpu as pltpu

D_MODEL = 1024
D_FF = 2816
EPS = 1e-6
DILATIONS = (1, 4, 16)
N_GROUPS = 3
HEADS_PER_GROUP = 8
HEAD_DIM = 64
GROUP_WIDTH = HEADS_PER_GROUP * HEAD_DIM
N_BUCKETS = 32
MAX_DISTANCE = 1024
NEG_INF = -1e30

HALF_KEYS = 64
LANES = 128
HALO = 16
ROW_TILE = 1024
QKV_TILE = 512
COL_CHUNK = 256
Q_TILE = 128
K_TILE = Q_TILE + 2 * HALF_KEYS
ATTN_ROWS = 1024
LOG2E = math.log2(math.e)
VMEM_LIMIT = 56 * 1024 * 1024

BF16 = jnp.bfloat16
F32 = jnp.float32


def _dot(a, b):
    return jnp.dot(a, b, preferred_element_type=F32)


def _rms(x, w):
    return x * lax.rsqrt(jnp.mean(x * x, axis=-1, keepdims=True) + EPS) * w


def _resident(shape):
    zeros = (0,) * len(shape)
    return pl.BlockSpec(shape, lambda *_: zeros, pipeline_mode=pl.Buffered(1))


def _seq_specs(seq, tile):
    per = tile // HALO
    last = seq // HALO - 1
    main = pl.BlockSpec((None, tile, D_MODEL), lambda b, i: (b, i, 0))
    prev = pl.BlockSpec((None, HALO, D_MODEL),
                        lambda b, i: (b, jnp.maximum(i * per - 1, 0), 0))
    nxt = pl.BlockSpec((None, HALO, D_MODEL),
                       lambda b, i: (b, jnp.minimum((i + 1) * per, last), 0))
    return main, prev, nxt


def _conv3(p, cw_ref, cols, tile):
    i = pl.program_id(1)
    rows = lax.broadcasted_iota(jnp.int32, (p.shape[0], 1), 0)
    outside = ((i == 0) & (rows < HALO)) | (
        (i == pl.num_programs(1) - 1) & (rows >= HALO + tile))
    p = jnp.where(outside, 0.0, p)
    n = p.shape[0]
    before = pltpu.roll(p, 1, axis=0)[HALO:HALO + tile]
    after = pltpu.roll(p, n - 1, axis=0)[HALO:HALO + tile]
    mid = p[HALO:HALO + tile]
    return (cw_ref[0:1, cols] * before + cw_ref[1:2, cols] * mid
            + cw_ref[2:3, cols] * after)


def _cast_plan(weights, grid):
    n_steps = grid[0] * grid[1]
    operands, in_specs, out_specs, out_shapes = [], [], [], []
    for w, layer in weights:
        _, rows, cols = w.shape
        n_blocks = next(n for n in (n_steps, n_steps // 2, n_steps // 4)
                        if rows % n == 0 and (rows // n) % HALO == 0)
        rep = n_steps // n_blocks

        def slab(b, i, rep=rep):
            return (b * grid[1] + i) // rep

        operands.append(w)
        in_specs.append(pl.BlockSpec((None, rows // n_blocks, cols),
                                     lambda b, i, layer=layer, slab=slab: (layer, slab(b, i), 0)))
        out_specs.append(pl.BlockSpec((rows // n_blocks, cols),
                                      lambda b, i, slab=slab: (slab(b, i), 0)))
        out_shapes.append(jax.ShapeDtypeStruct((rows, cols), BF16))
    return operands, in_specs, out_specs, out_shapes


def _cast_slabs(src_refs, dst_refs):
    for src, dst in zip(src_refs, dst_refs):
        dst[...] = src[...].astype(BF16)


def _mixer_kernel(*refs, n_cast):
    x_ref, xp_ref, xn_ref, nw_ref, win_ref, cw_ref, wout_ref = refs[:7]
    cast_src = refs[7:7 + n_cast]
    o_ref = refs[7 + n_cast]
    cast_dst = refs[8 + n_cast:8 + 2 * n_cast]
    hn_ref, a_ref = refs[8 + 2 * n_cast:]
    _cast_slabs(cast_src, cast_dst)
    tile = x_ref.shape[0]
    d = D_MODEL
    nw = nw_ref[...]
    hn_ref[0:HALO, :] = _rms(xp_ref[...], nw).astype(BF16)
    hn_ref[HALO:HALO + tile, :] = _rms(x_ref[...], nw).astype(BF16)
    hn_ref[HALO + tile:, :] = _rms(xn_ref[...], nw).astype(BF16)
    for j in range(d // COL_CHUNK):
        cols = slice(j * COL_CHUNK, (j + 1) * COL_CHUNK)
        gate = _dot(hn_ref[HALO:HALO + tile, :], win_ref[:, cols])
        c = _dot(hn_ref[...], win_ref[:, d + j * COL_CHUNK:d + (j + 1) * COL_CHUNK])
        h = _dot(hn_ref[...], win_ref[:, 2 * d + j * COL_CHUNK:2 * d + (j + 1) * COL_CHUNK])
        conv = _conv3(c * h, cw_ref, cols, tile)
        a_ref[:, cols] = (gate * conv).astype(BF16)
    o_ref[...] = x_ref[...] + _dot(a_ref[...], wout_ref[...])


def _ffn_kernel(*refs, has_branch, final_norm, n_cast):
    if has_branch:
        x_ref, xp_ref, xn_ref, y_ref, yp_ref, yn_ref = refs[:6]
        refs = refs[6:]
        tiles = [xp_ref[...] + yp_ref[...], x_ref[...] + y_ref[...], xn_ref[...] + yn_ref[...]]
    else:
        x_ref, xp_ref, xn_ref = refs[:3]
        refs = refs[3:]
        tiles = [xp_ref[...], x_ref[...], xn_ref[...]]
    nw_ref, wup_ref, cw_ref, cb_ref, wdown_ref, fw_ref = refs[:6]
    cast_src = refs[6:6 + n_cast]
    o_ref = refs[6 + n_cast]
    cast_dst = refs[7 + n_cast:7 + 2 * n_cast]
    hn_ref, a_ref = refs[7 + 2 * n_cast:]
    _cast_slabs(cast_src, cast_dst)
    tile = x_ref.shape[0]
    f = D_FF
    nw = nw_ref[...]
    hn_ref[0:HALO, :] = _rms(tiles[0], nw).astype(BF16)
    hn_ref[HALO:HALO + tile, :] = _rms(tiles[1], nw).astype(BF16)
    hn_ref[HALO + tile:, :] = _rms(tiles[2], nw).astype(BF16)
    for j in range(f // COL_CHUNK):
        cols = slice(j * COL_CHUNK, (j + 1) * COL_CHUNK)
        g = _dot(hn_ref[...], wup_ref[:, cols])
        u = _dot(hn_ref[HALO:HALO + tile, :],
                 wup_ref[:, f + j * COL_CHUNK:f + (j + 1) * COL_CHUNK])
        g = _conv3(g, cw_ref, cols, tile) + cb_ref[:, cols]
        act = g / (1.0 + jnp.exp(-g))
        a_ref[:, cols] = (act * u).astype(BF16)
    half = tile // 2
    for rows in (slice(0, half), slice(half, tile)):
        y = x_ref[rows, :] + _dot(a_ref[rows, :], wdown_ref[...])
        if has_branch:
            y = y + y_ref[rows, :]
        if final_norm:
            y = _rms(y, fw_ref[...])
        o_ref[rows, :] = y


def _qkv_kernel(x_ref, nw_ref, w_ref, o0_ref, o1_ref, o2_ref, hn_ref, hp_ref):
    tile = x_ref.shape[0]
    gw = GROUP_WIDTH
    hn = _rms(x_ref[...], nw_ref[...])
    for c in range(D_MODEL // LANES):
        hn_ref[c] = hn[:, c * LANES:(c + 1) * LANES]
    for g, (r, o_ref) in enumerate(zip(DILATIONS, (o0_ref, o1_ref, o2_ref))):
        rows = tile // r
        for p in range(r):
            for c in range(D_MODEL // LANES):
                hp_ref[g, p * rows:(p + 1) * rows, c * LANES:(c + 1) * LANES] = (
                    hn_ref[c, pl.ds(p, rows, stride=r), :].astype(BF16))
        for which in range(3):
            col = (which * N_GROUPS + g) * gw
            res = _dot(hp_ref[g], w_ref[:, col:col + gw])
            if which == 0:
                res = res * (HEAD_DIM ** -0.5 * LOG2E)
            for p in range(r):
                o_ref[p, :, which * gw:(which + 1) * gw] = (
                    res[p * rows:(p + 1) * rows].astype(BF16))


def _attn_kernel(q_ref, k_ref, v_ref, bias_ref, o_ref, m_ref, d_ref, *, phase_len):
    n_phase, n_rows, _ = q_ref.shape
    t = pl.program_id(2)
    lane = lax.broadcasted_iota(jnp.int32, (1, LANES), 1)
    low = lane < HEAD_DIM
    for ph in range(n_phase):
        for st in range(n_rows // Q_TILE):
            q0 = t * n_rows + st * Q_TILE
            start = jnp.clip(q0 - HALF_KEYS, 0, phase_len - K_TILE)
            start = pl.multiple_of(start, HALF_KEYS)
            variant = lax.shift_right_logical(q0 - start, 6)
            rows = slice(st * Q_TILE, (st + 1) * Q_TILE)
            m_all = jnp.zeros((Q_TILE, LANES), F32)
            d_all = jnp.ones((Q_TILE, LANES), F32)
            for hp in range(HEADS_PER_GROUP // 2):
                cols = slice(hp * LANES, (hp + 1) * LANES)
                q2 = q_ref[ph, rows, cols]
                k2 = k_ref[ph, pl.ds(start, K_TILE), cols]
                v2 = v_ref[ph, pl.ds(start, K_TILE), cols]
                zero = jnp.zeros_like(q2)
                qs = jnp.concatenate([jnp.where(low, q2, zero), jnp.where(low, zero, q2)], axis=0)
                s = lax.dot_general(qs, k2, (((1,), (1,)), ((), ())),
                                    preferred_element_type=F32)
                s = s + bias_ref[variant, hp]
                m = jnp.max(s, axis=-1, keepdims=True)
                e = jnp.exp2(s - m).astype(BF16)
                pv = _dot(e, jnp.concatenate([v2, jnp.ones_like(v2)], axis=1))
                num = pv[:, :LANES]
                den = pv[:, LANES:]
                o_ref[ph, rows, cols] = jnp.where(low, num[:Q_TILE], num[Q_TILE:]).astype(BF16)
                for half in range(2):
                    sel = lane == 2 * hp + half
                    part = slice(half * Q_TILE, (half + 1) * Q_TILE)
                    m_all = jnp.where(sel, m[part], m_all)
                    d_all = jnp.where(sel, den[part], d_all)
            m_ref[ph, rows, :] = m_all
            d_ref[ph, rows, :] = d_all


def _combine_kernel(o0_ref, o1_ref, o2_ref, m0_ref, m1_ref, m2_ref,
                    d0_ref, d1_ref, d2_ref, expand_ref, w_ref,
                    out_ref, on_ref, mn_ref, dn_ref, a_ref):
    tile = out_ref.shape[0]
    nchunk = GROUP_WIDTH // LANES
    groups = zip(DILATIONS, (o0_ref, o1_ref, o2_ref), (m0_ref, m1_ref, m2_ref),
                 (d0_ref, d1_ref, d2_ref))
    nums, maxes, dens = [], [], []
    for g, (r, o_ref, m_ref, d_ref) in enumerate(groups):
        if r == 1:
            nums.append([o_ref[0, :, c * LANES:(c + 1) * LANES].astype(F32)
                         for c in range(nchunk)])
            maxes.append(m_ref[0])
            dens.append(d_ref[0])
            continue
        rows = tile // r
        for p in range(r):
            o = o_ref[p].astype(F32)
            for c in range(nchunk):
                on_ref[g - 1, c, pl.ds(p, rows, stride=r), :] = o[:, c * LANES:(c + 1) * LANES]
            mn_ref[g - 1, pl.ds(p, rows, stride=r), :] = m_ref[p]
            dn_ref[g - 1, pl.ds(p, rows, stride=r), :] = d_ref[p]
        nums.append([on_ref[g - 1, c] for c in range(nchunk)])
        maxes.append(mn_ref[g - 1])
        dens.append(dn_ref[g - 1])
    m_max = jnp.maximum(jnp.maximum(maxes[0], maxes[1]), maxes[2])
    w = [jnp.exp2(m - m_max) for m in maxes]
    den = w[0] * dens[0] + w[1] * dens[1] + w[2] * dens[2]
    acc = [None] * nchunk
    for g in range(N_GROUPS):
        alpha = w[g] / den
        hi = alpha.astype(BF16)
        lo = (alpha - hi.astype(F32)).astype(BF16)
        wide = _dot(jnp.concatenate([hi, lo], axis=1), expand_ref[...])
        for c in range(nchunk):
            term = wide[:, c * LANES:(c + 1) * LANES] * nums[g][c]
            acc[c] = term if g == 0 else acc[c] + term
    for c in range(nchunk):
        a_ref[:, c * LANES:(c + 1) * LANES] = acc[c].astype(BF16)
    out_ref[...] = _dot(a_ref[...], w_ref[...])


def _params(n_axes=2):
    return pltpu.CompilerParams(
        dimension_semantics=("arbitrary",) * n_axes, vmem_limit_bytes=VMEM_LIMIT)


def _mixer(x, nw, w_in, cw, w_out, cast=()):
    b, s, d = x.shape
    grid = (b, s // ROW_TILE)
    main, prev, nxt = _seq_specs(s, ROW_TILE)
    c_ops, c_in, c_out, c_shapes = _cast_plan(cast, grid)
    return pl.pallas_call(
        functools.partial(_mixer_kernel, n_cast=len(cast)),
        grid=grid,
        in_specs=[main, prev, nxt, _resident((1, d)), _resident((d, 3 * d)),
                  _resident((3, d)), _resident((d, d))] + c_in,
        out_specs=[main] + c_out,
        out_shape=[jax.ShapeDtypeStruct(x.shape, F32)] + c_shapes,
        scratch_shapes=[pltpu.VMEM((ROW_TILE + 2 * HALO, d), BF16),
                        pltpu.VMEM((ROW_TILE, d), BF16)],
        compiler_params=_params(),
        name="short_conv_mixer",
    )(x, x, x, nw, w_in, cw, w_out, *c_ops)


def _ffn(x, branch, nw, w_up, cw, cb, w_down, fw, final_norm, cast=()):
    b, s, d = x.shape
    f = D_FF
    grid = (b, s // ROW_TILE)
    seq = list(_seq_specs(s, ROW_TILE))
    has_branch = branch is not None
    acts = [x, x, x] + ([branch, branch, branch] if has_branch else [])
    c_ops, c_in, c_out, c_shapes = _cast_plan(cast, grid)
    return pl.pallas_call(
        functools.partial(_ffn_kernel, has_branch=has_branch, final_norm=final_norm,
                          n_cast=len(cast)),
        grid=grid,
        in_specs=seq * (2 if has_branch else 1) + [
            _resident((1, d)), _resident((d, 2 * f)), _resident((3, f)), _resident((1, f)),
            _resident((f, d)), _resident((1, d))] + c_in,
        out_specs=[seq[0]] + c_out,
        out_shape=[jax.ShapeDtypeStruct(x.shape, F32)] + c_shapes,
        scratch_shapes=[pltpu.VMEM((ROW_TILE + 2 * HALO, d), BF16),
                        pltpu.VMEM((ROW_TILE, f), BF16)],
        compiler_params=_params(),
        name="conv_ffn_final" if final_norm else "conv_ffn",
    )(*acts, nw, w_up, cw, cb, w_down, fw, *c_ops)


def _qkv(x, nw, w):
    b, s, d = x.shape
    gw3 = 3 * GROUP_WIDTH
    out_shapes, out_specs = [], []
    for r in DILATIONS:
        out_shapes.append(jax.ShapeDtypeStruct((b, r, s // r, gw3), BF16))
        out_specs.append(pl.BlockSpec((None, r, QKV_TILE // r, gw3),
                                      lambda bi, i: (bi, 0, i, 0)))
    return pl.pallas_call(
        _qkv_kernel,
        grid=(b, s // QKV_TILE),
        in_specs=[pl.BlockSpec((None, QKV_TILE, d), lambda bi, i: (bi, i, 0)),
                  _resident((1, d)), _resident((d, N_GROUPS * gw3))],
        out_specs=out_specs,
        out_shape=out_shapes,
        scratch_shapes=[pltpu.VMEM((d // LANES, QKV_TILE, LANES), F32),
                        pltpu.VMEM((N_GROUPS, QKV_TILE, d), BF16)],
        compiler_params=_params(),
        name="qkv_proj",
    )(x, nw, w)


def _attention(qkv, bias, r):
    b, _, phase_len, _ = qkv.shape
    gw = GROUP_WIDTH
    n_rows = min(ATTN_ROWS, phase_len)
    n_phase = ATTN_ROWS // n_rows
    q_spec = pl.BlockSpec((None, n_phase, n_rows, gw), lambda bi, p, t: (bi, p, t, 0))
    k_spec = pl.BlockSpec((None, n_phase, phase_len, gw), lambda bi, p, t: (bi, p, 0, 1))
    v_spec = pl.BlockSpec((None, n_phase, phase_len, gw), lambda bi, p, t: (bi, p, 0, 2))
    stat_spec = pl.BlockSpec((None, n_phase, n_rows, LANES), lambda bi, p, t: (bi, p, t, 0))
    stat_shape = jax.ShapeDtypeStruct((b, r, phase_len, LANES), F32)
    return pl.pallas_call(
        functools.partial(_attn_kernel, phase_len=phase_len),
        grid=(b, r // n_phase, phase_len // n_rows),
        in_specs=[q_spec, k_spec, v_spec, _resident(bias.shape)],
        out_specs=[q_spec, stat_spec, stat_spec],
        out_shape=[jax.ShapeDtypeStruct((b, r, phase_len, gw), BF16), stat_shape, stat_shape],
        compiler_params=_params(3),
        name=f"local_attn_d{r}",
    )(qkv, qkv, qkv, bias)


def _combine(outs, maxes, dens, w_out):
    b = outs[0].shape[0]
    s = outs[0].shape[1] * outs[0].shape[2]
    d = D_MODEL
    gw = GROUP_WIDTH
    o_specs = [pl.BlockSpec((None, r, ROW_TILE // r, gw), lambda bi, i: (bi, 0, i, 0))
               for r in DILATIONS]
    stat_specs = [pl.BlockSpec((None, r, ROW_TILE // r, LANES), lambda bi, i: (bi, 0, i, 0))
                  for r in DILATIONS]
    expand = np.zeros((2 * LANES, gw), np.float32)
    for h in range(HEADS_PER_GROUP):
        expand[h, h * HEAD_DIM:(h + 1) * HEAD_DIM] = 1.0
        expand[LANES + h, h * HEAD_DIM:(h + 1) * HEAD_DIM] = 1.0
    return pl.pallas_call(
        _combine_kernel,
        grid=(b, s // ROW_TILE),
        in_specs=o_specs + stat_specs + stat_specs
        + [_resident((2 * LANES, gw)), _resident((gw, d))],
        out_specs=pl.BlockSpec((None, ROW_TILE, d), lambda bi, i: (bi, i, 0)),
        out_shape=jax.ShapeDtypeStruct((b, s, d), F32),
        scratch_shapes=[pltpu.VMEM((N_GROUPS - 1, gw // LANES, ROW_TILE, LANES), F32),
                        pltpu.VMEM((N_GROUPS - 1, ROW_TILE, LANES), F32),
                        pltpu.VMEM((N_GROUPS - 1, ROW_TILE, LANES), F32),
                        pltpu.VMEM((ROW_TILE, gw), BF16)],
        compiler_params=_params(),
        name="attn_combine_out",
    )(*outs, *maxes, *dens, jnp.asarray(expand, BF16), w_out)


def _t5_bucket(rel):
    half = N_BUCKETS // 2
    max_exact = half // 2
    n = jnp.abs(rel)
    side = jnp.where(rel > 0, half, 0)
    nf = jnp.maximum(n, 1).astype(jnp.float32)
    large = max_exact + (jnp.log(nf / max_exact) / math.log(MAX_DISTANCE / max_exact)
                         * (half - max_exact)).astype(jnp.int32)
    large = jnp.minimum(large, half - 1)
    return side + jnp.where(n < max_exact, n, large)


def _bias_tiles(rel_bias, g):
    r = DILATIONS[g]
    heads = HEADS_PER_GROUP
    offs = r * jnp.arange(-HALF_KEYS, HALF_KEYS + 1, dtype=jnp.int32)
    onehot = _t5_bucket(offs)[:, None] == jnp.arange(N_BUCKETS)[None, :]
    table = rel_bias[:, g * heads:(g + 1) * heads].astype(F32)
    per_offset = jnp.sum(jnp.where(onehot[:, :, None], table[None], 0.0), axis=1) * LOG2E
    n = 4 * Q_TILE
    row = jnp.full((heads, n + 1), NEG_INF, F32)
    row = row.at[:, 2 * Q_TILE - HALF_KEYS:2 * Q_TILE + HALF_KEYS + 1].set(per_offset.T)
    skew = jnp.tile(row, (1, Q_TILE))[:, :Q_TILE * n].reshape(heads, Q_TILE, n)
    tiles = jnp.stack([skew[:, :, 2 * Q_TILE - HALF_KEYS * v:2 * Q_TILE - HALF_KEYS * v + K_TILE]
                       for v in range(3)])
    return tiles.reshape(3, heads // 2, 2 * Q_TILE, K_TILE)


def kernel(x, norm_w, conv_in, conv_w, conv_out, attn_qkv, attn_out, rel_bias,
           ffn_up, ffn_conv_w, ffn_conv_b, ffn_down, final_norm):
    d = D_MODEL
    fw = final_norm.reshape(1, d)

    x, w_up0, w_down0 = _mixer(
        x, norm_w[0, 0].reshape(1, d), conv_in[0].astype(BF16), conv_w[0],
        conv_out[0].astype(BF16), cast=((ffn_up, 0), (ffn_down, 0)))
    x, w_qkv, w_attn_out, w_up1, w_down1 = _ffn(
        x, None, norm_w[0, 1].reshape(1, d), w_up0, ffn_conv_w[0],
        ffn_conv_b[0].reshape(1, D_FF), w_down0, fw, False,
        cast=((attn_qkv, 0), (attn_out, 0), (ffn_up, 1), (ffn_down, 1)))

    qkvs = _qkv(x, norm_w[1, 0].reshape(1, d), w_qkv)
    outs, maxes, dens = [], [], []
    for g, r in enumerate(DILATIONS):
        o, m, den = _attention(qkvs[g], _bias_tiles(rel_bias, g), r)
        outs.append(o)
        maxes.append(m)
        dens.append(den)
    branch = _combine(outs, maxes, dens, w_attn_out)
    (x,) = _ffn(x, branch, norm_w[1, 1].reshape(1, d), w_up1, ffn_conv_w[1],
                ffn_conv_b[1].reshape(1, D_FF), w_down1, fw, True)
    return x
```

```python
import functools
import math

import jax
import jax.numpy as jnp
import numpy as np
from jax import lax
from jax.experimental import pallas as pl
from jax.experimental.pallas import tpu as pltpu

D_MODEL = 1024
D_FF = 2816
EPS = 1e-6
DILATIONS = (1, 4, 16)
N_GROUPS = 3
HEADS_PER_GROUP = 8
HEAD_DIM = 64
GROUP_WIDTH = HEADS_PER_GROUP * HEAD_DIM
N_BUCKETS = 32
MAX_DISTANCE = 1024
NEG_INF = -1e30

HALF_KEYS = 64
LANES = 128
HALO = 16
ROW_TILE = 1024
QKV_TILE = 1024
COL_CHUNK = 256
Q_TILE = 128
K_TILE = Q_TILE + 2 * HALF_KEYS
ATTN_ROWS = 2048
LOG2E = math.log2(math.e)
VMEM_LIMIT = 56 * 1024 * 1024

BF16 = jnp.bfloat16
F32 = jnp.float32


def _dot(a, b):
    return jnp.dot(a, b, preferred_element_type=F32)


def _rms(x, w):
    return x * lax.rsqrt(jnp.mean(x * x, axis=-1, keepdims=True) + EPS) * w


def _resident(shape):
    zeros = (0,) * len(shape)
    return pl.BlockSpec(shape, lambda *_: zeros, pipeline_mode=pl.Buffered(1))


def _seq_specs(seq, tile):
    per = tile // HALO
    last = seq // HALO - 1
    main = pl.BlockSpec((None, tile, D_MODEL), lambda b, i: (b, i, 0))
    prev = pl.BlockSpec((None, HALO, D_MODEL),
                        lambda b, i: (b, jnp.maximum(i * per - 1, 0), 0))
    nxt = pl.BlockSpec((None, HALO, D_MODEL),
                       lambda b, i: (b, jnp.minimum((i + 1) * per, last), 0))
    return main, prev, nxt


def _conv3(p, cw_ref, cols, tile):
    i = pl.program_id(1)
    rows = lax.broadcasted_iota(jnp.int32, (p.shape[0], 1), 0)
    outside = ((i == 0) & (rows < HALO)) | (
        (i == pl.num_programs(1) - 1) & (rows >= HALO + tile))
    p = jnp.where(outside, 0.0, p)
    n = p.shape[0]
    before = pltpu.roll(p, 1, axis=0)[HALO:HALO + tile]
    after = pltpu.roll(p, n - 1, axis=0)[HALO:HALO + tile]
    mid = p[HALO:HALO + tile]
    return (cw_ref[0:1, cols] * before + cw_ref[1:2, cols] * mid
            + cw_ref[2:3, cols] * after)


def _cast_plan(weights, grid):
    n_steps = grid[0] * grid[1]
    operands, in_specs, out_specs, out_shapes = [], [], [], []
    for w, layer in weights:
        _, rows, cols = w.shape
        n_blocks = next(n for n in (n_steps, n_steps // 2, n_steps // 4)
                        if rows % n == 0 and (rows // n) % HALO == 0)
        rep = n_steps // n_blocks

        def slab(b, i, rep=rep):
            return (b * grid[1] + i) // rep

        operands.append(w)
        in_specs.append(pl.BlockSpec((None, rows // n_blocks, cols),
                                     lambda b, i, layer=layer, slab=slab: (layer, slab(b, i), 0)))
        out_specs.append(pl.BlockSpec((rows // n_blocks, cols),
                                      lambda b, i, slab=slab: (slab(b, i), 0)))
        out_shapes.append(jax.ShapeDtypeStruct((rows, cols), BF16))
    return operands, in_specs, out_specs, out_shapes


def _cast_slabs(src_refs, dst_refs):
    for src, dst in zip(src_refs, dst_refs):
        dst[...] = src[...].astype(BF16)


def _mixer_kernel(*refs, n_cast):
    x_ref, xp_ref, xn_ref, nw_ref, win_ref, cw_ref, wout_ref = refs[:7]
    cast_src = refs[7:7 + n_cast]
    o_ref = refs[7 + n_cast]
    cast_dst = refs[8 + n_cast:8 + 2 * n_cast]
    hn_ref, a_ref = refs[8 + 2 * n_cast:]
    _cast_slabs(cast_src, cast_dst)
    tile = x_ref.shape[0]
    d = D_MODEL
    nw = nw_ref[...]
    hn_ref[0:HALO, :] = _rms(xp_ref[...], nw).astype(BF16)
    hn_ref[HALO:HALO + tile, :] = _rms(x_ref[...], nw).astype(BF16)
    hn_ref[HALO + tile:, :] = _rms(xn_ref[...], nw).astype(BF16)
    for j in range(d // COL_CHUNK):
        cols = slice(j * COL_CHUNK, (j + 1) * COL_CHUNK)
        gate = _dot(hn_ref[HALO:HALO + tile, :], win_ref[:, cols])
        c = _dot(hn_ref[...], win_ref[:, d + j * COL_CHUNK:d + (j + 1) * COL_CHUNK])
        h = _dot(hn_ref[...], win_ref[:, 2 * d + j * COL_CHUNK:2 * d + (j + 1) * COL_CHUNK])
        conv = _conv3(c * h, cw_ref, cols, tile)
        a_ref[:, cols] = (gate * conv).astype(BF16)
    o_ref[...] = x_ref[...] + _dot(a_ref[...], wout_ref[...])


def _ffn_kernel(*refs, has_branch, final_norm, n_cast):
    if has_branch:
        x_ref, xp_ref, xn_ref, y_ref, yp_ref, yn_ref = refs[:6]
        refs = refs[6:]
        tiles = [xp_ref[...] + yp_ref[...], x_ref[...] + y_ref[...], xn_ref[...] + yn_ref[...]]
    else:
        x_ref, xp_ref, xn_ref = refs[:3]
        refs = refs[3:]
        tiles = [xp_ref[...], x_ref[...], xn_ref[...]]
    nw_ref, wup_ref, cw_ref, cb_ref, wdown_ref, fw_ref = refs[:6]
    cast_src = refs[6:6 + n_cast]
    o_ref = refs[6 + n_cast]
    cast_dst = refs[7 + n_cast:7 + 2 * n_cast]
    hn_ref, a_ref = refs[7 + 2 * n_cast:]
    _cast_slabs(cast_src, cast_dst)
    tile = x_ref.shape[0]
    f = D_FF
    nw = nw_ref[...]
    hn_ref[0:HALO, :] = _rms(tiles[0], nw).astype(BF16)
    hn_ref[HALO:HALO + tile, :] = _rms(tiles[1], nw).astype(BF16)
    hn_ref[HALO + tile:, :] = _rms(tiles[2], nw).astype(BF16)
    for j in range(f // COL_CHUNK):
        cols = slice(j * COL_CHUNK, (j + 1) * COL_CHUNK)
        g = _dot(hn_ref[...], wup_ref[:, cols])
        u = _dot(hn_ref[HALO:HALO + tile, :],
                 wup_ref[:, f + j * COL_CHUNK:f + (j + 1) * COL_CHUNK])
        g = _conv3(g, cw_ref, cols, tile) + cb_ref[:, cols]
        act = g / (1.0 + jnp.exp(-g))
        a_ref[:, cols] = (act * u).astype(BF16)
    half = tile // 2
    for rows in (slice(0, half), slice(half, tile)):
        y = x_ref[rows, :] + _dot(a_ref[rows, :], wdown_ref[...])
        if has_branch:
            y = y + y_ref[rows, :]
        if final_norm:
            y = _rms(y, fw_ref[...])
        o_ref[rows, :] = y


def _qkv_kernel(x_ref, nw_ref, w_ref, o0_ref, o1_ref, o2_ref, hn_ref, hp_ref):
    tile = x_ref.shape[0]
    gw = GROUP_WIDTH
    hn = _rms(x_ref[...], nw_ref[...])
    for c in range(D_MODEL // LANES):
        hn_ref[c] = hn[:, c * LANES:(c + 1) * LANES]
    for g, (r, o_ref) in enumerate(zip(DILATIONS, (o0_ref, o1_ref, o2_ref))):
        rows = tile // r
        for p in range(r):
            for c in range(D_MODEL // LANES):
                hp_ref[g, p * rows:(p + 1) * rows, c * LANES:(c + 1) * LANES] = (
                    hn_ref[c, pl.ds(p, rows, stride=r), :].astype(BF16))
        for which in range(3):
            col = (which * N_GROUPS + g) * gw
            res = _dot(hp_ref[g], w_ref[:, col:col + gw])
            if which == 0:
                res = res * (HEAD_DIM ** -0.5 * LOG2E)
            for p in range(r):
                o_ref[p, :, which * gw:(which + 1) * gw] = (
                    res[p * rows:(p + 1) * rows].astype(BF16))


def _attn_kernel(q_ref, k_ref, v_ref, bias_ref, o_ref, m_ref, d_ref, *, phase_len):
    n_phase, n_rows, _ = q_ref.shape
    t = pl.program_id(2)
    lane = lax.broadcasted_iota(jnp.int32, (1, LANES), 1)
    low = lane < HEAD_DIM
    for ph in range(n_phase):
        for st in range(n_rows // Q_TILE):
            q0 = t * n_rows + st * Q_TILE
            start = jnp.clip(q0 - HALF_KEYS, 0, phase_len - K_TILE)
            start = pl.multiple_of(start, HALF_KEYS)
            variant = lax.shift_right_logical(q0 - start, 6)
            rows = slice(st * Q_TILE, (st + 1) * Q_TILE)
            m_all = jnp.zeros((Q_TILE, LANES), F32)
            d_all = jnp.ones((Q_TILE, LANES), F32)
            for hp in range(HEADS_PER_GROUP // 2):
                cols = slice(hp * LANES, (hp + 1) * LANES)
                q2 = q_ref[ph, rows, cols]
                k2 = k_ref[ph, pl.ds(start, K_TILE), cols]
                v2 = v_ref[ph, pl.ds(start, K_TILE), cols]
                zero = jnp.zeros_like(q2)
                qs = jnp.concatenate([jnp.where(low, q2, zero), jnp.where(low, zero, q2)], axis=0)
                s = lax.dot_general(qs, k2, (((1,), (1,)), ((), ())),
                                    preferred_element_type=F32)
                s = s + bias_ref[variant, hp]
                m = jnp.max(s, axis=-1, keepdims=True)
                e = jnp.exp2(s - m).astype(BF16)
                pv = _dot(e, jnp.concatenate([v2, jnp.ones_like(v2)], axis=1))
                num = pv[:, :LANES]
                den = pv[:, LANES:]
                o_ref[ph, rows, cols] = jnp.where(low, num[:Q_TILE], num[Q_TILE:]).astype(BF16)
                for half in range(2):
                    sel = lane == 2 * hp + half
                    part = slice(half * Q_TILE, (half + 1) * Q_TILE)
                    m_all = jnp.where(sel, m[part], m_all)
                    d_all = jnp.where(sel, den[part], d_all)
            m_ref[ph, rows, :] = m_all
            d_ref[ph, rows, :] = d_all


def _interleave(dst, src_of_phase, r, tile, tmp):
    rows = tile // r
    if r <= 4:
        for p in range(r):
            dst[pl.ds(p, rows, stride=r), :] = src_of_phase(p)
        return
    inner = r // 4
    for p in range(r):
        hi, lo = divmod(p, 4)
        tmp[pl.ds(lo * (tile // 4) + hi, rows, stride=inner), :] = src_of_phase(p)
    for lo in range(4):
        dst[pl.ds(lo, tile // 4, stride=4), :] = tmp[lo * (tile // 4):(lo + 1) * (tile // 4), :]


def _combine_kernel(o0_ref, o1_ref, o2_ref, m0_ref, m1_ref, m2_ref,
                    d0_ref, d1_ref, d2_ref, expand_ref, w_ref,
                    out_ref, on_ref, mn_ref, dn_ref, tmp_ref, a_ref):
    tile = out_ref.shape[0]
    nchunk = GROUP_WIDTH // LANES
    groups = zip(DILATIONS, (o0_ref, o1_ref, o2_ref), (m0_ref, m1_ref, m2_ref),
                 (d0_ref, d1_ref, d2_ref))
    nums, maxes, dens = [], [], []
    for g, (r, o_ref, m_ref, d_ref) in enumerate(groups):
        if r == 1:
            nums.append([o_ref[0, :, c * LANES:(c + 1) * LANES].astype(F32)
                         for c in range(nchunk)])
            maxes.append(m_ref[0])
            dens.append(d_ref[0])
            continue
        for c in range(nchunk):
            _interleave(on_ref.at[g - 1, c],
                        lambda p, c=c: o_ref[p, :, c * LANES:(c + 1) * LANES].astype(F32),
                        r, tile, tmp_ref.at[c])
        _interleave(mn_ref.at[g - 1], lambda p: m_ref[p], r, tile, tmp_ref.at[nchunk])
        _interleave(dn_ref.at[g - 1], lambda p: d_ref[p], r, tile, tmp_ref.at[nchunk + 1])
        nums.append([on_ref[g - 1, c] for c in range(nchunk)])
        maxes.append(mn_ref[g - 1])
        dens.append(dn_ref[g - 1])
    m_max = jnp.maximum(jnp.maximum(maxes[0], maxes[1]), maxes[2])
    w = [jnp.exp2(m - m_max) for m in maxes]
    den = w[0] * dens[0] + w[1] * dens[1] + w[2] * dens[2]
    acc = [None] * nchunk
    for g in range(N_GROUPS):
        alpha = w[g] / den
        hi = alpha.astype(BF16)
        lo = (alpha - hi.astype(F32)).astype(BF16)
        wide = _dot(jnp.concatenate([hi, lo], axis=1), expand_ref[...])
        for c in range(nchunk):
            term = wide[:, c * LANES:(c + 1) * LANES] * nums[g][c]
            acc[c] = term if g == 0 else acc[c] + term
    for c in range(nchunk):
        a_ref[:, c * LANES:(c + 1) * LANES] = acc[c].astype(BF16)
    out_ref[...] = _dot(a_ref[...], w_ref[...])


def _params(n_axes=2):
    return pltpu.CompilerParams(
        dimension_semantics=("arbitrary",) * n_axes, vmem_limit_bytes=VMEM_LIMIT)


def _mixer(x, nw, w_in, cw, w_out, cast=()):
    b, s, d = x.shape
    grid = (b, s // ROW_TILE)
    main, prev, nxt = _seq_specs(s, ROW_TILE)
    c_ops, c_in, c_out, c_shapes = _cast_plan(cast, grid)
    return pl.pallas_call(
        functools.partial(_mixer_kernel, n_cast=len(cast)),
        grid=grid,
        in_specs=[main, prev, nxt, _resident((1, d)), _resident((d, 3 * d)),
                  _resident((3, d)), _resident((d, d))] + c_in,
        out_specs=[main] + c_out,
        out_shape=[jax.ShapeDtypeStruct(x.shape, F32)] + c_shapes,
        scratch_shapes=[pltpu.VMEM((ROW_TILE + 2 * HALO, d), BF16),
                        pltpu.VMEM((ROW_TILE, d), BF16)],
        compiler_params=_params(),
        name="short_conv_mixer",
    )(x, x, x, nw, w_in, cw, w_out, *c_ops)


def _ffn(x, branch, nw, w_up, cw, cb, w_down, fw, final_norm, cast=()):
    b, s, d = x.shape
    f = D_FF
    grid = (b, s // ROW_TILE)
    seq = list(_seq_specs(s, ROW_TILE))
    has_branch = branch is not None
    acts = [x, x, x] + ([branch, branch, branch] if has_branch else [])
    c_ops, c_in, c_out, c_shapes = _cast_plan(cast, grid)
    return pl.pallas_call(
        functools.partial(_ffn_kernel, has_branch=has_branch, final_norm=final_norm,
                          n_cast=len(cast)),
        grid=grid,
        in_specs=seq * (2 if has_branch else 1) + [
            _resident((1, d)), _resident((d, 2 * f)), _resident((3, f)), _resident((1, f)),
            _resident((f, d)), _resident((1, d))] + c_in,
        out_specs=[seq[0]] + c_out,
        out_shape=[jax.ShapeDtypeStruct(x.shape, F32)] + c_shapes,
        scratch_shapes=[pltpu.VMEM((ROW_TILE + 2 * HALO, d), BF16),
                        pltpu.VMEM((ROW_TILE, f), BF16)],
        compiler_params=_params(),
        name="conv_ffn_final" if final_norm else "conv_ffn",
    )(*acts, nw, w_up, cw, cb, w_down, fw, *c_ops)


def _qkv(x, nw, w):
    b, s, d = x.shape
    gw3 = 3 * GROUP_WIDTH
    out_shapes, out_specs = [], []
    for r in DILATIONS:
        out_shapes.append(jax.ShapeDtypeStruct((b, r, s // r, gw3), BF16))
        out_specs.append(pl.BlockSpec((None, r, QKV_TILE // r, gw3),
                                      lambda bi, i: (bi, 0, i, 0)))
    return pl.pallas_call(
        _qkv_kernel,
        grid=(b, s // QKV_TILE),
        in_specs=[pl.BlockSpec((None, QKV_TILE, d), lambda bi, i: (bi, i, 0)),
                  _resident((1, d)), _resident((d, N_GROUPS * gw3))],
        out_specs=out_specs,
        out_shape=out_shapes,
        scratch_shapes=[pltpu.VMEM((d // LANES, QKV_TILE, LANES), F32),
                        pltpu.VMEM((N_GROUPS, QKV_TILE, d), BF16)],
        compiler_params=_params(),
        name="qkv_proj",
    )(x, nw, w)


def _attention(qkv, bias, r):
    b, _, phase_len, _ = qkv.shape
    gw = GROUP_WIDTH
    n_rows = min(ATTN_ROWS, phase_len)
    n_phase = ATTN_ROWS // n_rows
    q_spec = pl.BlockSpec((None, n_phase, n_rows, gw), lambda bi, p, t: (bi, p, t, 0))
    k_spec = pl.BlockSpec((None, n_phase, phase_len, gw), lambda bi, p, t: (bi, p, 0, 1))
    v_spec = pl.BlockSpec((None, n_phase, phase_len, gw), lambda bi, p, t: (bi, p, 0, 2))
    stat_spec = pl.BlockSpec((None, n_phase, n_rows, LANES), lambda bi, p, t: (bi, p, t, 0))
    stat_shape = jax.ShapeDtypeStruct((b, r, phase_len, LANES), F32)
    return pl.pallas_call(
        functools.partial(_attn_kernel, phase_len=phase_len),
        grid=(b, r // n_phase, phase_len // n_rows),
        in_specs=[q_spec, k_spec, v_spec, _resident(bias.shape)],
        out_specs=[q_spec, stat_spec, stat_spec],
        out_shape=[jax.ShapeDtypeStruct((b, r, phase_len, gw), BF16), stat_shape, stat_shape],
        compiler_params=_params(3),
        name=f"local_attn_d{r}",
    )(qkv, qkv, qkv, bias)


def _combine(outs, maxes, dens, w_out):
    b = outs[0].shape[0]
    s = outs[0].shape[1] * outs[0].shape[2]
    d = D_MODEL
    gw = GROUP_WIDTH
    o_specs = [pl.BlockSpec((None, r, ROW_TILE // r, gw), lambda bi, i: (bi, 0, i, 0))
               for r in DILATIONS]
    stat_specs = [pl.BlockSpec((None, r, ROW_TILE // r, LANES), lambda bi, i: (bi, 0, i, 0))
                  for r in DILATIONS]
    expand = np.zeros((2 * LANES, gw), np.float32)
    for h in range(HEADS_PER_GROUP):
        expand[h, h * HEAD_DIM:(h + 1) * HEAD_DIM] = 1.0
        expand[LANES + h, h * HEAD_DIM:(h + 1) * HEAD_DIM] = 1.0
    return pl.pallas_call(
        _combine_kernel,
        grid=(b, s // ROW_TILE),
        in_specs=o_specs + stat_specs + stat_specs
        + [_resident((2 * LANES, gw)), _resident((gw, d))],
        out_specs=pl.BlockSpec((None, ROW_TILE, d), lambda bi, i: (bi, i, 0)),
        out_shape=jax.ShapeDtypeStruct((b, s, d), F32),
        scratch_shapes=[pltpu.VMEM((N_GROUPS - 1, gw // LANES, ROW_TILE, LANES), F32),
                        pltpu.VMEM((N_GROUPS - 1, ROW_TILE, LANES), F32),
                        pltpu.VMEM((N_GROUPS - 1, ROW_TILE, LANES), F32),
                        pltpu.VMEM((gw // LANES + 2, ROW_TILE, LANES), F32),
                        pltpu.VMEM((ROW_TILE, gw), BF16)],
        compiler_params=_params(),
        name="attn_combine_out",
    )(*outs, *maxes, *dens, jnp.asarray(expand, BF16), w_out)


def _t5_bucket(rel):
    half = N_BUCKETS // 2
    max_exact = half // 2
    n = jnp.abs(rel)
    side = jnp.where(rel > 0, half, 0)
    nf = jnp.maximum(n, 1).astype(jnp.float32)
    large = max_exact + (jnp.log(nf / max_exact) / math.log(MAX_DISTANCE / max_exact)
                         * (half - max_exact)).astype(jnp.int32)
    large = jnp.minimum(large, half - 1)
    return side + jnp.where(n < max_exact, n, large)


def _bias_tiles(rel_bias, g):
    r = DILATIONS[g]
    heads = HEADS_PER_GROUP
    offs = r * jnp.arange(-HALF_KEYS, HALF_KEYS + 1, dtype=jnp.int32)
    onehot = _t5_bucket(offs)[:, None] == jnp.arange(N_BUCKETS)[None, :]
    table = rel_bias[:, g * heads:(g + 1) * heads].astype(F32)
    per_offset = jnp.sum(jnp.where(onehot[:, :, None], table[None], 0.0), axis=1) * LOG2E
    n = 4 * Q_TILE
    row = jnp.full((heads, n + 1), NEG_INF, F32)
    row = row.at[:, 2 * Q_TILE - HALF_KEYS:2 * Q_TILE + HALF_KEYS + 1].set(per_offset.T)
    skew = jnp.tile(row, (1, Q_TILE))[:, :Q_TILE * n].reshape(heads, Q_TILE, n)
    tiles = jnp.stack([skew[:, :, 2 * Q_TILE - HALF_KEYS * v:2 * Q_TILE - HALF_KEYS * v + K_TILE]
                       for v in range(3)])
    return tiles.reshape(3, heads // 2, 2 * Q_TILE, K_TILE)


def kernel(x, norm_w, conv_in, conv_w, conv_out, attn_qkv, attn_out, rel_bias,
           ffn_up, ffn_conv_w, ffn_conv_b, ffn_down, final_norm):
    d = D_MODEL
    fw = final_norm.reshape(1, d)

    x, w_up0, w_down0 = _mixer(
        x, norm_w[0, 0].reshape(1, d), conv_in[0].astype(BF16), conv_w[0],
        conv_out[0].astype(BF16), cast=((ffn_up, 0), (ffn_down, 0)))
    x, w_qkv, w_attn_out, w_up1, w_down1 = _ffn(
        x, None, norm_w[0, 1].reshape(1, d), w_up0, ffn_conv_w[0],
        ffn_conv_b[0].reshape(1, D_FF), w_down0, fw, False,
        cast=((attn_qkv, 0), (attn_out, 0), (ffn_up, 1), (ffn_down, 1)))

    qkvs = _qkv(x, norm_w[1, 0].reshape(1, d), w_qkv)
    outs, maxes, dens = [], [], []
    for g, r in enumerate(DILATIONS):
        o, m, den = _attention(qkvs[g], _bias_tiles(rel_bias, g), r)
        outs.append(o)
        maxes.append(m)
        dens.append(den)
    branch = _combine(outs, maxes, dens, w_attn_out)
    (x,) = _ffn(x, branch, norm_w[1, 1].reshape(1, d), w_up1, ffn_conv_w[1],
                ffn_conv_b[1].reshape(1, D_FF), w_down1, fw, True)
    return x
```

```python
import functools
import math

import jax
import jax.numpy as jnp
import numpy as np
from jax import lax
from jax.experimental import pallas as pl
from jax.experimental.pallas import tpu as pltpu

D_MODEL = 1024
D_FF = 2816
EPS = 1e-6
DILATIONS = (1, 4, 16)
N_GROUPS = 3
HEADS_PER_GROUP = 8
HEAD_DIM = 64
GROUP_WIDTH = HEADS_PER_GROUP * HEAD_DIM
N_BUCKETS = 32
MAX_DISTANCE = 1024
NEG_INF = -1e30

HALF_KEYS = 64
LANES = 128
HALO = 16
ROW_TILE = 1024
QKV_TILE = 1024
COL_CHUNK = 256
Q_TILE = 128
K_TILE = Q_TILE + 2 * HALF_KEYS
ATTN_ROWS = 2048
LOG2E = math.log2(math.e)
VMEM_LIMIT = 56 * 1024 * 1024

BF16 = jnp.bfloat16
F32 = jnp.float32


def _dot(a, b):
    return jnp.dot(a, b, preferred_element_type=F32)


def _rms(x, w):
    return x * lax.rsqrt(jnp.mean(x * x, axis=-1, keepdims=True) + EPS) * w


def _resident(shape):
    zeros = (0,) * len(shape)
    return pl.BlockSpec(shape, lambda *_: zeros, pipeline_mode=pl.Buffered(1))


def _seq_specs(seq, tile):
    per = tile // HALO
    last = seq // HALO - 1
    main = pl.BlockSpec((None, tile, D_MODEL), lambda b, i: (b, i, 0))
    prev = pl.BlockSpec((None, HALO, D_MODEL),
                        lambda b, i: (b, jnp.maximum(i * per - 1, 0), 0))
    nxt = pl.BlockSpec((None, HALO, D_MODEL),
                       lambda b, i: (b, jnp.minimum((i + 1) * per, last), 0))
    return main, prev, nxt


def _conv3(p, cw_ref, cols, tile):
    i = pl.program_id(1)
    rows = lax.broadcasted_iota(jnp.int32, (p.shape[0], 1), 0)
    outside = ((i == 0) & (rows < HALO)) | (
        (i == pl.num_programs(1) - 1) & (rows >= HALO + tile))
    p = jnp.where(outside, 0.0, p)
    n = p.shape[0]
    before = pltpu.roll(p, 1, axis=0)[HALO:HALO + tile]
    after = pltpu.roll(p, n - 1, axis=0)[HALO:HALO + tile]
    mid = p[HALO:HALO + tile]
    return (cw_ref[0:1, cols] * before + cw_ref[1:2, cols] * mid
            + cw_ref[2:3, cols] * after)


def _cast_plan(weights, grid):
    n_steps = grid[0] * grid[1]
    operands, in_specs, out_specs, out_shapes = [], [], [], []
    for w, layer in weights:
        _, rows, cols = w.shape
        n_blocks = next(n for n in (n_steps, n_steps // 2, n_steps // 4)
                        if rows % n == 0 and (rows // n) % HALO == 0)
        rep = n_steps // n_blocks

        def slab(b, i, rep=rep):
            return (b * grid[1] + i) // rep

        operands.append(w)
        in_specs.append(pl.BlockSpec((None, rows // n_blocks, cols),
                                     lambda b, i, layer=layer, slab=slab: (layer, slab(b, i), 0)))
        out_specs.append(pl.BlockSpec((rows // n_blocks, cols),
                                      lambda b, i, slab=slab: (slab(b, i), 0)))
        out_shapes.append(jax.ShapeDtypeStruct((rows, cols), BF16))
    return operands, in_specs, out_specs, out_shapes


def _cast_slabs(src_refs, dst_refs):
    for src, dst in zip(src_refs, dst_refs):
        dst[...] = src[...].astype(BF16)


def _mixer_kernel(*refs, n_cast):
    x_ref, xp_ref, xn_ref, nw_ref, win_ref, cw_ref, wout_ref = refs[:7]
    cast_src = refs[7:7 + n_cast]
    o_ref = refs[7 + n_cast]
    cast_dst = refs[8 + n_cast:8 + 2 * n_cast]
    hn_ref, a_ref = refs[8 + 2 * n_cast:]
    _cast_slabs(cast_src, cast_dst)
    tile = x_ref.shape[0]
    d = D_MODEL
    nw = nw_ref[...]
    hn_ref[0:HALO, :] = _rms(xp_ref[...], nw).astype(BF16)
    hn_ref[HALO:HALO + tile, :] = _rms(x_ref[...], nw).astype(BF16)
    hn_ref[HALO + tile:, :] = _rms(xn_ref[...], nw).astype(BF16)
    for j in range(d // COL_CHUNK):
        cols = slice(j * COL_CHUNK, (j + 1) * COL_CHUNK)
        gate = _dot(hn_ref[HALO:HALO + tile, :], win_ref[:, cols])
        c = _dot(hn_ref[...], win_ref[:, d + j * COL_CHUNK:d + (j + 1) * COL_CHUNK])
        h = _dot(hn_ref[...], win_ref[:, 2 * d + j * COL_CHUNK:2 * d + (j + 1) * COL_CHUNK])
        conv = _conv3(c * h, cw_ref, cols, tile)
        a_ref[:, cols] = (gate * conv).astype(BF16)
    o_ref[...] = x_ref[...] + _dot(a_ref[...], wout_ref[...])


def _ffn_kernel(*refs, has_branch, final_norm, n_cast):
    if has_branch:
        x_ref, xp_ref, xn_ref, y_ref, yp_ref, yn_ref = refs[:6]
        refs = refs[6:]
        tiles = [xp_ref[...] + yp_ref[...], x_ref[...] + y_ref[...], xn_ref[...] + yn_ref[...]]
    else:
        x_ref, xp_ref, xn_ref = refs[:3]
        refs = refs[3:]
        tiles = [xp_ref[...], x_ref[...], xn_ref[...]]
    nw_ref, wup_ref, cw_ref, cb_ref, wdown_ref, fw_ref = refs[:6]
    cast_src = refs[6:6 + n_cast]
    o_ref = refs[6 + n_cast]
    cast_dst = refs[7 + n_cast:7 + 2 * n_cast]
    hn_ref, a_ref = refs[7 + 2 * n_cast:]
    _cast_slabs(cast_src, cast_dst)
    tile = x_ref.shape[0]
    f = D_FF
    nw = nw_ref[...]
    hn_ref[0:HALO, :] = _rms(tiles[0], nw).astype(BF16)
    hn_ref[HALO:HALO + tile, :] = _rms(tiles[1], nw).astype(BF16)
    hn_ref[HALO + tile:, :] = _rms(tiles[2], nw).astype(BF16)
    for j in range(f // COL_CHUNK):
        cols = slice(j * COL_CHUNK, (j + 1) * COL_CHUNK)
        g = _dot(hn_ref[...], wup_ref[:, cols])
        u = _dot(hn_ref[HALO:HALO + tile, :],
                 wup_ref[:, f + j * COL_CHUNK:f + (j + 1) * COL_CHUNK])
        g = _conv3(g, cw_ref, cols, tile) + cb_ref[:, cols]
        act = g / (1.0 + jnp.exp(-g))
        a_ref[:, cols] = (act * u).astype(BF16)
    half = tile // 2
    for rows in (slice(0, half), slice(half, tile)):
        y = x_ref[rows, :] + _dot(a_ref[rows, :], wdown_ref[...])
        if has_branch:
            y = y + y_ref[rows, :]
        if final_norm:
            y = _rms(y, fw_ref[...])
        o_ref[rows, :] = y


def _qkv_kernel(x_ref, nw_ref, w_ref, o0_ref, o1_ref, o2_ref, hn_ref, h4_ref, hp_ref):
    tile = x_ref.shape[0]
    gw = GROUP_WIDTH
    nchunk = D_MODEL // LANES
    assert DILATIONS == (1, 4, 16)
    hn = _rms(x_ref[...], nw_ref[...])
    hp_ref[0] = hn.astype(BF16)
    for c in range(nchunk):
        hn_ref[c] = hn[:, c * LANES:(c + 1) * LANES]
    quarter = tile // 4
    for p in range(4):
        for c in range(nchunk):
            blk = hn_ref[c, pl.ds(p, quarter, stride=4), :]
            h4_ref[c, p * quarter:(p + 1) * quarter, :] = blk
            hp_ref[1, p * quarter:(p + 1) * quarter, c * LANES:(c + 1) * LANES] = blk.astype(BF16)
    rows16 = tile // 16
    for p in range(16):
        start = (p % 4) * quarter + p // 4
        for c in range(nchunk):
            hp_ref[2, p * rows16:(p + 1) * rows16, c * LANES:(c + 1) * LANES] = (
                h4_ref[c, pl.ds(start, rows16, stride=4), :].astype(BF16))
    for g, (r, o_ref) in enumerate(zip(DILATIONS, (o0_ref, o1_ref, o2_ref))):
        rows = tile // r
        for which in range(3):
            col = (which * N_GROUPS + g) * gw
            res = _dot(hp_ref[g], w_ref[:, col:col + gw])
            if which == 0:
                res = res * (HEAD_DIM ** -0.5 * LOG2E)
            for p in range(r):
                o_ref[p, :, which * gw:(which + 1) * gw] = (
                    res[p * rows:(p + 1) * rows].astype(BF16))


def _attn_kernel(q_ref, k_ref, v_ref, bias_ref, o_ref, m_ref, d_ref, *, phase_len):
    n_phase, n_rows, _ = q_ref.shape
    t = pl.program_id(2)
    lane = lax.broadcasted_iota(jnp.int32, (1, LANES), 1)
    low = lane < HEAD_DIM
    for ph in range(n_phase):
        for st in range(n_rows // Q_TILE):
            q0 = t * n_rows + st * Q_TILE
            start = jnp.clip(q0 - HALF_KEYS, 0, phase_len - K_TILE)
            start = pl.multiple_of(start, HALF_KEYS)
            variant = lax.shift_right_logical(q0 - start, 6)
            rows = slice(st * Q_TILE, (st + 1) * Q_TILE)
            m_all = jnp.zeros((Q_TILE, LANES), F32)
            d_all = jnp.ones((Q_TILE, LANES), F32)
            for hp in range(HEADS_PER_GROUP // 2):
                cols = slice(hp * LANES, (hp + 1) * LANES)
                q2 = q_ref[ph, rows, cols]
                k2 = k_ref[ph, pl.ds(start, K_TILE), cols]
                v2 = v_ref[ph, pl.ds(start, K_TILE), cols]
                zero = jnp.zeros_like(q2)
                qs = jnp.concatenate([jnp.where(low, q2, zero), jnp.where(low, zero, q2)], axis=0)
                s = lax.dot_general(qs, k2, (((1,), (1,)), ((), ())),
                                    preferred_element_type=F32)
                s = s + bias_ref[variant, hp]
                m = jnp.max(s, axis=-1, keepdims=True)
                e = jnp.exp2(s - m).astype(BF16)
                pv = _dot(e, jnp.concatenate([v2, jnp.ones_like(v2)], axis=1))
                num = pv[:, :LANES]
                den = pv[:, LANES:]
                o_ref[ph, rows, cols] = jnp.where(low, num[:Q_TILE], num[Q_TILE:]).astype(BF16)
                for half in range(2):
                    sel = lane == 2 * hp + half
                    part = slice(half * Q_TILE, (half + 1) * Q_TILE)
                    m_all = jnp.where(sel, m[part], m_all)
                    d_all = jnp.where(sel, den[part], d_all)
            m_ref[ph, rows, :] = m_all
            d_ref[ph, rows, :] = d_all


def _interleave(dst, src_of_phase, r, tile, tmp):
    rows = tile // r
    if r <= 4:
        for p in range(r):
            dst[pl.ds(p, rows, stride=r), :] = src_of_phase(p)
        return
    inner = r // 4
    for p in range(r):
        hi, lo = divmod(p, 4)
        tmp[pl.ds(lo * (tile // 4) + hi, rows, stride=inner), :] = src_of_phase(p)
    for lo in range(4):
        dst[pl.ds(lo, tile // 4, stride=4), :] = tmp[lo * (tile // 4):(lo + 1) * (tile // 4), :]


def _combine_kernel(o0_ref, o1_ref, o2_ref, m0_ref, m1_ref, m2_ref,
                    d0_ref, d1_ref, d2_ref, expand_ref, w_ref,
                    out_ref, on_ref, mn_ref, dn_ref, tmp_ref, a_ref):
    tile = out_ref.shape[0]
    nchunk = GROUP_WIDTH // LANES
    groups = zip(DILATIONS, (o0_ref, o1_ref, o2_ref), (m0_ref, m1_ref, m2_ref),
                 (d0_ref, d1_ref, d2_ref))
    nums, maxes, dens = [], [], []
    for g, (r, o_ref, m_ref, d_ref) in enumerate(groups):
        if r == 1:
            nums.append([o_ref[0, :, c * LANES:(c + 1) * LANES].astype(F32)
                         for c in range(nchunk)])
            maxes.append(m_ref[0])
            dens.append(d_ref[0])
            continue
        for c in range(nchunk):
            _interleave(on_ref.at[g - 1, c],
                        lambda p, c=c: o_ref[p, :, c * LANES:(c + 1) * LANES].astype(F32),
                        r, tile, tmp_ref.at[c])
        _interleave(mn_ref.at[g - 1], lambda p: m_ref[p], r, tile, tmp_ref.at[nchunk])
        _interleave(dn_ref.at[g - 1], lambda p: d_ref[p], r, tile, tmp_ref.at[nchunk + 1])
        nums.append([on_ref[g - 1, c] for c in range(nchunk)])
        maxes.append(mn_ref[g - 1])
        dens.append(dn_ref[g - 1])
    m_max = jnp.maximum(jnp.maximum(maxes[0], maxes[1]), maxes[2])
    w = [jnp.exp2(m - m_max) for m in maxes]
    den = w[0] * dens[0] + w[1] * dens[1] + w[2] * dens[2]
    acc = [None] * nchunk
    for g in range(N_GROUPS):
        alpha = w[g] / den
        hi = alpha.astype(BF16)
        lo = (alpha - hi.astype(F32)).astype(BF16)
        wide = _dot(jnp.concatenate([hi, lo], axis=1), expand_ref[...])
        for c in range(nchunk):
            term = wide[:, c * LANES:(c + 1) * LANES] * nums[g][c]
            acc[c] = term if g == 0 else acc[c] + term
    for c in range(nchunk):
        a_ref[:, c * LANES:(c + 1) * LANES] = acc[c].astype(BF16)
    out_ref[...] = _dot(a_ref[...], w_ref[...])


def _params(n_axes=2):
    return pltpu.CompilerParams(
        dimension_semantics=("arbitrary",) * n_axes, vmem_limit_bytes=VMEM_LIMIT)


def _mixer(x, nw, w_in, cw, w_out, cast=()):
    b, s, d = x.shape
    grid = (b, s // ROW_TILE)
    main, prev, nxt = _seq_specs(s, ROW_TILE)
    c_ops, c_in, c_out, c_shapes = _cast_plan(cast, grid)
    return pl.pallas_call(
        functools.partial(_mixer_kernel, n_cast=len(cast)),
        grid=grid,
        in_specs=[main, prev, nxt, _resident((1, d)), _resident((d, 3 * d)),
                  _resident((3, d)), _resident((d, d))] + c_in,
        out_specs=[main] + c_out,
        out_shape=[jax.ShapeDtypeStruct(x.shape, F32)] + c_shapes,
        scratch_shapes=[pltpu.VMEM((ROW_TILE + 2 * HALO, d), BF16),
                        pltpu.VMEM((ROW_TILE, d), BF16)],
        compiler_params=_params(),
        name="short_conv_mixer",
    )(x, x, x, nw, w_in, cw, w_out, *c_ops)


def _ffn(x, branch, nw, w_up, cw, cb, w_down, fw, final_norm, cast=()):
    b, s, d = x.shape
    f = D_FF
    grid = (b, s // ROW_TILE)
    seq = list(_seq_specs(s, ROW_TILE))
    has_branch = branch is not None
    acts = [x, x, x] + ([branch, branch, branch] if has_branch else [])
    c_ops, c_in, c_out, c_shapes = _cast_plan(cast, grid)
    return pl.pallas_call(
        functools.partial(_ffn_kernel, has_branch=has_branch, final_norm=final_norm,
                          n_cast=len(cast)),
        grid=grid,
        in_specs=seq * (2 if has_branch else 1) + [
            _resident((1, d)), _resident((d, 2 * f)), _resident((3, f)), _resident((1, f)),
            _resident((f, d)), _resident((1, d))] + c_in,
        out_specs=[seq[0]] + c_out,
        out_shape=[jax.ShapeDtypeStruct(x.shape, F32)] + c_shapes,
        scratch_shapes=[pltpu.VMEM((ROW_TILE + 2 * HALO, d), BF16),
                        pltpu.VMEM((ROW_TILE, f), BF16)],
        compiler_params=_params(),
        name="conv_ffn_final" if final_norm else "conv_ffn",
    )(*acts, nw, w_up, cw, cb, w_down, fw, *c_ops)


def _qkv(x, nw, w):
    b, s, d = x.shape
    gw3 = 3 * GROUP_WIDTH
    out_shapes, out_specs = [], []
    for r in DILATIONS:
        out_shapes.append(jax.ShapeDtypeStruct((b, r, s // r, gw3), BF16))
        out_specs.append(pl.BlockSpec((None, r, QKV_TILE // r, gw3),
                                      lambda bi, i: (bi, 0, i, 0)))
    return pl.pallas_call(
        _qkv_kernel,
        grid=(b, s // QKV_TILE),
        in_specs=[pl.BlockSpec((None, QKV_TILE, d), lambda bi, i: (bi, i, 0)),
                  _resident((1, d)), _resident((d, N_GROUPS * gw3))],
        out_specs=out_specs,
        out_shape=out_shapes,
        scratch_shapes=[pltpu.VMEM((d // LANES, QKV_TILE, LANES), F32),
                        pltpu.VMEM((d // LANES, QKV_TILE, LANES), F32),
                        pltpu.VMEM((N_GROUPS, QKV_TILE, d), BF16)],
        compiler_params=_params(),
        name="qkv_proj",
    )(x, nw, w)


def _attention(qkv, bias, g):
    b, r, phase_len, _ = qkv.shape
    gw = GROUP_WIDTH
    pairs = HEADS_PER_GROUP // 2
    bias_spec = pl.BlockSpec((3, pairs, 2 * Q_TILE, K_TILE), lambda *_: (0, g, 0, 0),
                             pipeline_mode=pl.Buffered(1))
    n_rows = min(ATTN_ROWS, phase_len)
    n_phase = ATTN_ROWS // n_rows
    q_spec = pl.BlockSpec((None, n_phase, n_rows, gw), lambda bi, p, t: (bi, p, t, 0))
    k_spec = pl.BlockSpec((None, n_phase, phase_len, gw), lambda bi, p, t: (bi, p, 0, 1))
    v_spec = pl.BlockSpec((None, n_phase, phase_len, gw), lambda bi, p, t: (bi, p, 0, 2))
    stat_spec = pl.BlockSpec((None, n_phase, n_rows, LANES), lambda bi, p, t: (bi, p, t, 0))
    stat_shape = jax.ShapeDtypeStruct((b, r, phase_len, LANES), F32)
    return pl.pallas_call(
        functools.partial(_attn_kernel, phase_len=phase_len),
        grid=(b, r // n_phase, phase_len // n_rows),
        in_specs=[q_spec, k_spec, v_spec, bias_spec],
        out_specs=[q_spec, stat_spec, stat_spec],
        out_shape=[jax.ShapeDtypeStruct((b, r, phase_len, gw), BF16), stat_shape, stat_shape],
        compiler_params=_params(3),
        name=f"local_attn_d{r}",
    )(qkv, qkv, qkv, bias)


def _combine(outs, maxes, dens, w_out):
    b = outs[0].shape[0]
    s = outs[0].shape[1] * outs[0].shape[2]
    d = D_MODEL
    gw = GROUP_WIDTH
    o_specs = [pl.BlockSpec((None, r, ROW_TILE // r, gw), lambda bi, i: (bi, 0, i, 0))
               for r in DILATIONS]
    stat_specs = [pl.BlockSpec((None, r, ROW_TILE // r, LANES), lambda bi, i: (bi, 0, i, 0))
                  for r in DILATIONS]
    expand = np.zeros((2 * LANES, gw), np.float32)
    for h in range(HEADS_PER_GROUP):
        expand[h, h * HEAD_DIM:(h + 1) * HEAD_DIM] = 1.0
        expand[LANES + h, h * HEAD_DIM:(h + 1) * HEAD_DIM] = 1.0
    return pl.pallas_call(
        _combine_kernel,
        grid=(b, s // ROW_TILE),
        in_specs=o_specs + stat_specs + stat_specs
        + [_resident((2 * LANES, gw)), _resident((gw, d))],
        out_specs=pl.BlockSpec((None, ROW_TILE, d), lambda bi, i: (bi, i, 0)),
        out_shape=jax.ShapeDtypeStruct((b, s, d), F32),
        scratch_shapes=[pltpu.VMEM((N_GROUPS - 1, gw // LANES, ROW_TILE, LANES), F32),
                        pltpu.VMEM((N_GROUPS - 1, ROW_TILE, LANES), F32),
                        pltpu.VMEM((N_GROUPS - 1, ROW_TILE, LANES), F32),
                        pltpu.VMEM((gw // LANES + 2, ROW_TILE, LANES), F32),
                        pltpu.VMEM((ROW_TILE, gw), BF16)],
        compiler_params=_params(),
        name="attn_combine_out",
    )(*outs, *maxes, *dens, jnp.asarray(expand, BF16), w_out)


def _t5_bucket(rel):
    half = N_BUCKETS // 2
    max_exact = half // 2
    n = jnp.abs(rel)
    side = jnp.where(rel > 0, half, 0)
    nf = jnp.maximum(n, 1).astype(jnp.float32)
    large = max_exact + (jnp.log(nf / max_exact) / math.log(MAX_DISTANCE / max_exact)
                         * (half - max_exact)).astype(jnp.int32)
    large = jnp.minimum(large, half - 1)
    return side + jnp.where(n < max_exact, n, large)


def _bias_tiles(rel_bias):
    heads = N_GROUPS * HEADS_PER_GROUP
    steps = jnp.arange(-HALF_KEYS, HALF_KEYS + 1, dtype=jnp.int32)
    offs = jnp.asarray(DILATIONS, jnp.int32)[:, None] * steps[None, :]
    onehot = _t5_bucket(offs)[:, :, None] == jnp.arange(N_BUCKETS)[None, None, :]
    table = rel_bias.astype(F32).reshape(N_BUCKETS, N_GROUPS, HEADS_PER_GROUP)
    table = jnp.transpose(table, (1, 0, 2))
    per_offset = jnp.sum(jnp.where(onehot[:, :, :, None], table[:, None], 0.0), axis=2) * LOG2E
    per_offset = jnp.transpose(per_offset, (0, 2, 1)).reshape(heads, 2 * HALF_KEYS + 1)
    n = 4 * Q_TILE
    row = jnp.full((heads, n + 1), NEG_INF, F32)
    row = row.at[:, 2 * Q_TILE - HALF_KEYS:2 * Q_TILE + HALF_KEYS + 1].set(per_offset)
    skew = jnp.tile(row, (1, Q_TILE))[:, :Q_TILE * n].reshape(heads, Q_TILE, n)
    tiles = jnp.stack([skew[:, :, 2 * Q_TILE - HALF_KEYS * v:2 * Q_TILE - HALF_KEYS * v + K_TILE]
                       for v in range(3)])
    return tiles.reshape(3, heads // 2, 2 * Q_TILE, K_TILE)


def kernel(x, norm_w, conv_in, conv_w, conv_out, attn_qkv, attn_out, rel_bias,
           ffn_up, ffn_conv_w, ffn_conv_b, ffn_down, final_norm):
    d = D_MODEL
    fw = final_norm.reshape(1, d)

    x, w_up0, w_down0 = _mixer(
        x, norm_w[0, 0].reshape(1, d), conv_in[0].astype(BF16), conv_w[0],
        conv_out[0].astype(BF16), cast=((ffn_up, 0), (ffn_down, 0)))
    x, w_qkv, w_attn_out, w_up1, w_down1 = _ffn(
        x, None, norm_w[0, 1].reshape(1, d), w_up0, ffn_conv_w[0],
        ffn_conv_b[0].reshape(1, D_FF), w_down0, fw, False,
        cast=((attn_qkv, 0), (attn_out, 0), (ffn_up, 1), (ffn_down, 1)))

    qkvs = _qkv(x, norm_w[1, 0].reshape(1, d), w_qkv)
    bias = _bias_tiles(rel_bias)
    outs, maxes, dens = [], [], []
    for g in range(N_GROUPS):
        o, m, den = _attention(qkvs[g], bias, g)
        outs.append(o)
        maxes.append(m)
        dens.append(den)
    branch = _combine(outs, maxes, dens, w_attn_out)
    (x,) = _ffn(x, branch, norm_w[1, 1].reshape(1, d), w_up1, ffn_conv_w[1],
                ffn_conv_b[1].reshape(1, D_FF), w_down1, fw, True)
    return x
```

```python
import functools
import math

import jax
import jax.numpy as jnp
import numpy as np
from jax import lax
from jax.experimental import pallas as pl
from jax.experimental.pallas import tpu as pltpu

D_MODEL = 1024
D_FF = 2816
EPS = 1e-6
DILATIONS = (1, 4, 16)
N_GROUPS = 3
HEADS_PER_GROUP = 8
HEAD_DIM = 64
GROUP_WIDTH = HEADS_PER_GROUP * HEAD_DIM
N_BUCKETS = 32
MAX_DISTANCE = 1024
NEG_INF = -1e30

HALF_KEYS = 64
LANES = 128
HALO = 16
ROW_TILE = 1024
QKV_TILE = 1024
COL_CHUNK = 256
Q_TILE = 128
K_TILE = Q_TILE + 2 * HALF_KEYS
ATTN_ROWS = 4096
LOG2E = math.log2(math.e)
V7X_VMEM_BYTES = 64 * 1024 * 1024
VMEM_LIMIT = V7X_VMEM_BYTES - 8 * 1024 * 1024

BF16 = jnp.bfloat16
F32 = jnp.float32


def _dot(a, b):
    return jnp.dot(a, b, preferred_element_type=F32)


def _rms(x, w):
    return x * lax.rsqrt(jnp.mean(x * x, axis=-1, keepdims=True) + EPS) * w


def _resident(shape):
    zeros = (0,) * len(shape)
    return pl.BlockSpec(shape, lambda *_: zeros, pipeline_mode=pl.Buffered(1))


def _seq_specs(seq, tile):
    per = tile // HALO
    last = seq // HALO - 1
    main = pl.BlockSpec((None, tile, D_MODEL), lambda b, i: (b, i, 0))
    prev = pl.BlockSpec((None, HALO, D_MODEL),
                        lambda b, i: (b, jnp.maximum(i * per - 1, 0), 0))
    nxt = pl.BlockSpec((None, HALO, D_MODEL),
                       lambda b, i: (b, jnp.minimum((i + 1) * per, last), 0))
    return main, prev, nxt


def _conv3(p, cw_ref, cols, tile):
    i = pl.program_id(1)
    rows = lax.broadcasted_iota(jnp.int32, (p.shape[0], 1), 0)
    outside = ((i == 0) & (rows < HALO)) | (
        (i == pl.num_programs(1) - 1) & (rows >= HALO + tile))
    p = jnp.where(outside, 0.0, p)
    n = p.shape[0]
    before = pltpu.roll(p, 1, axis=0)[HALO:HALO + tile]
    after = pltpu.roll(p, n - 1, axis=0)[HALO:HALO + tile]
    mid = p[HALO:HALO + tile]
    return (cw_ref[0:1, cols] * before + cw_ref[1:2, cols] * mid
            + cw_ref[2:3, cols] * after)


def _cast_plan(weights, grid):
    n_steps = grid[0] * grid[1]
    operands, in_specs, out_specs, out_shapes = [], [], [], []
    for w, layer in weights:
        _, rows, cols = w.shape
        n_blocks = next(n for n in (n_steps, n_steps // 2, n_steps // 4)
                        if rows % n == 0 and (rows // n) % HALO == 0)
        rep = n_steps // n_blocks

        def slab(b, i, rep=rep):
            return (b * grid[1] + i) // rep

        operands.append(w)
        in_specs.append(pl.BlockSpec((None, rows // n_blocks, cols),
                                     lambda b, i, layer=layer, slab=slab: (layer, slab(b, i), 0)))
        out_specs.append(pl.BlockSpec((rows // n_blocks, cols),
                                      lambda b, i, slab=slab: (slab(b, i), 0)))
        out_shapes.append(jax.ShapeDtypeStruct((rows, cols), BF16))
    return operands, in_specs, out_specs, out_shapes


def _cast_slabs(src_refs, dst_refs):
    for src, dst in zip(src_refs, dst_refs):
        dst[...] = src[...].astype(BF16)


def _mixer_kernel(*refs, n_cast):
    x_ref, xp_ref, xn_ref, nw_ref, win_ref, cw_ref, wout_ref = refs[:7]
    cast_src = refs[7:7 + n_cast]
    o_ref = refs[7 + n_cast]
    cast_dst = refs[8 + n_cast:8 + 2 * n_cast]
    hn_ref, a_ref = refs[8 + 2 * n_cast:]
    _cast_slabs(cast_src, cast_dst)
    tile = x_ref.shape[0]
    d = D_MODEL
    nw = nw_ref[...]
    hn_ref[0:HALO, :] = _rms(xp_ref[...], nw).astype(BF16)
    hn_ref[HALO:HALO + tile, :] = _rms(x_ref[...], nw).astype(BF16)
    hn_ref[HALO + tile:, :] = _rms(xn_ref[...], nw).astype(BF16)
    for j in range(d // COL_CHUNK):
        cols = slice(j * COL_CHUNK, (j + 1) * COL_CHUNK)
        gate = _dot(hn_ref[HALO:HALO + tile, :], win_ref[:, cols])
        c = _dot(hn_ref[...], win_ref[:, d + j * COL_CHUNK:d + (j + 1) * COL_CHUNK])
        h = _dot(hn_ref[...], win_ref[:, 2 * d + j * COL_CHUNK:2 * d + (j + 1) * COL_CHUNK])
        conv = _conv3(c * h, cw_ref, cols, tile)
        a_ref[:, cols] = (gate * conv).astype(BF16)
    o_ref[...] = x_ref[...] + _dot(a_ref[...], wout_ref[...])


def _ffn_kernel(*refs, has_branch, final_norm, n_cast):
    if has_branch:
        x_ref, xp_ref, xn_ref, y_ref, yp_ref, yn_ref = refs[:6]
        refs = refs[6:]
        tiles = [xp_ref[...] + yp_ref[...], x_ref[...] + y_ref[...], xn_ref[...] + yn_ref[...]]
    else:
        x_ref, xp_ref, xn_ref = refs[:3]
        refs = refs[3:]
        tiles = [xp_ref[...], x_ref[...], xn_ref[...]]
    nw_ref, wup_ref, cw_ref, cb_ref, wdown_ref, fw_ref = refs[:6]
    cast_src = refs[6:6 + n_cast]
    o_ref = refs[6 + n_cast]
    cast_dst = refs[7 + n_cast:7 + 2 * n_cast]
    hn_ref, a_ref = refs[7 + 2 * n_cast:]
    _cast_slabs(cast_src, cast_dst)
    tile = x_ref.shape[0]
    f = D_FF
    nw = nw_ref[...]
    hn_ref[0:HALO, :] = _rms(tiles[0], nw).astype(BF16)
    hn_ref[HALO:HALO + tile, :] = _rms(tiles[1], nw).astype(BF16)
    hn_ref[HALO + tile:, :] = _rms(tiles[2], nw).astype(BF16)
    for j in range(f // COL_CHUNK):
        cols = slice(j * COL_CHUNK, (j + 1) * COL_CHUNK)
        g = _dot(hn_ref[...], wup_ref[:, cols])
        u = _dot(hn_ref[HALO:HALO + tile, :],
                 wup_ref[:, f + j * COL_CHUNK:f + (j + 1) * COL_CHUNK])
        g = _conv3(g, cw_ref, cols, tile) + cb_ref[:, cols]
        act = g / (1.0 + jnp.exp(-g))
        a_ref[:, cols] = (act * u).astype(BF16)
    half = tile // 2
    for rows in (slice(0, half), slice(half, tile)):
        y = x_ref[rows, :] + _dot(a_ref[rows, :], wdown_ref[...])
        if has_branch:
            y = y + y_ref[rows, :]
        if final_norm:
            y = _rms(y, fw_ref[...])
        o_ref[rows, :] = y


def _qkv_kernel(x_ref, nw_ref, w_ref, o0_ref, o1_ref, o2_ref, hn_ref, hp_ref):
    tile = x_ref.shape[0]
    gw = GROUP_WIDTH
    hn = _rms(x_ref[...], nw_ref[...])
    for c in range(D_MODEL // LANES):
        hn_ref[c] = hn[:, c * LANES:(c + 1) * LANES]
    for g, (r, o_ref) in enumerate(zip(DILATIONS, (o0_ref, o1_ref, o2_ref))):
        rows = tile // r
        for p in range(r):
            for c in range(D_MODEL // LANES):
                hp_ref[g, p * rows:(p + 1) * rows, c * LANES:(c + 1) * LANES] = (
                    hn_ref[c, pl.ds(p, rows, stride=r), :].astype(BF16))
        for which in range(3):
            col = (which * N_GROUPS + g) * gw
            res = _dot(hp_ref[g], w_ref[:, col:col + gw])
            if which == 0:
                res = res * (HEAD_DIM ** -0.5 * LOG2E)
            for p in range(r):
                o_ref[p, :, which * gw:(which + 1) * gw] = (
                    res[p * rows:(p + 1) * rows].astype(BF16))


def _attn_kernel(q_ref, k_ref, v_ref, bias_ref, o_ref, m_ref, d_ref, *, phase_len,
                 key_origin=0):
    n_phase, n_rows, _ = q_ref.shape
    t = pl.program_id(2)
    lane = lax.broadcasted_iota(jnp.int32, (1, LANES), 1)
    low = lane < HEAD_DIM
    for ph in range(n_phase):
        for st in range(n_rows // Q_TILE):
            q0 = t * n_rows + st * Q_TILE
            first_key = jnp.clip(q0 - HALF_KEYS, 0, phase_len - K_TILE)
            variant = lax.shift_right_logical(q0 - first_key, 6)
            start = pl.multiple_of(first_key - key_origin, HALF_KEYS)
            rows = slice(st * Q_TILE, (st + 1) * Q_TILE)
            m_all = jnp.zeros((Q_TILE, LANES), F32)
            d_all = jnp.ones((Q_TILE, LANES), F32)
            for hp in range(HEADS_PER_GROUP // 2):
                cols = slice(hp * LANES, (hp + 1) * LANES)
                q2 = q_ref[ph, rows, cols]
                k2 = k_ref[ph, pl.ds(start, K_TILE), cols]
                v2 = v_ref[ph, pl.ds(start, K_TILE), cols]
                zero = jnp.zeros_like(q2)
                qs = jnp.concatenate([jnp.where(low, q2, zero), jnp.where(low, zero, q2)], axis=0)
                s = lax.dot_general(qs, k2, (((1,), (1,)), ((), ())),
                                    preferred_element_type=F32)
                s = s + bias_ref[variant, hp]
                m = jnp.max(s, axis=-1, keepdims=True)
                e = jnp.exp2(s - m).astype(BF16)
                pv = _dot(e, jnp.concatenate([v2, jnp.ones_like(v2)], axis=1))
                num = pv[:, :LANES]
                den = pv[:, LANES:]
                o_ref[ph, rows, cols] = jnp.where(low, num[:Q_TILE], num[Q_TILE:]).astype(BF16)
                for half in range(2):
                    sel = lane == 2 * hp + half
                    part = slice(half * Q_TILE, (half + 1) * Q_TILE)
                    m_all = jnp.where(sel, m[part], m_all)
                    d_all = jnp.where(sel, den[part], d_all)
            m_ref[ph, rows, :] = m_all
            d_ref[ph, rows, :] = d_all


def _interleave(dst, src_of_phase, r, tile, tmp):
    rows = tile // r
    if r <= 4:
        for p in range(r):
            dst[pl.ds(p, rows, stride=r), :] = src_of_phase(p)
        return
    inner = r // 4
    for p in range(r):
        hi, lo = divmod(p, 4)
        tmp[pl.ds(lo * (tile // 4) + hi, rows, stride=inner), :] = src_of_phase(p)
    for lo in range(4):
        dst[pl.ds(lo, tile // 4, stride=4), :] = tmp[lo * (tile // 4):(lo + 1) * (tile // 4), :]


def _combine_kernel(o0_ref, o1_ref, o2_ref, m0_ref, m1_ref, m2_ref,
                    d0_ref, d1_ref, d2_ref, expand_ref, w_ref,
                    out_ref, on_ref, mn_ref, dn_ref, tmp_ref, a_ref):
    tile = out_ref.shape[0]
    nchunk = GROUP_WIDTH // LANES
    groups = zip(DILATIONS, (o0_ref, o1_ref, o2_ref), (m0_ref, m1_ref, m2_ref),
                 (d0_ref, d1_ref, d2_ref))
    nums, maxes, dens = [], [], []
    for g, (r, o_ref, m_ref, d_ref) in enumerate(groups):
        if r == 1:
            nums.append([o_ref[0, :, c * LANES:(c + 1) * LANES].astype(F32)
                         for c in range(nchunk)])
            maxes.append(m_ref[0])
            dens.append(d_ref[0])
            continue
        for c in range(nchunk):
            _interleave(on_ref.at[g - 1, c],
                        lambda p, c=c: o_ref[p, :, c * LANES:(c + 1) * LANES].astype(F32),
                        r, tile, tmp_ref.at[c])
        _interleave(mn_ref.at[g - 1], lambda p: m_ref[p], r, tile, tmp_ref.at[nchunk])
        _interleave(dn_ref.at[g - 1], lambda p: d_ref[p], r, tile, tmp_ref.at[nchunk + 1])
        nums.append([on_ref[g - 1, c] for c in range(nchunk)])
        maxes.append(mn_ref[g - 1])
        dens.append(dn_ref[g - 1])
    m_max = jnp.maximum(jnp.maximum(maxes[0], maxes[1]), maxes[2])
    w = [jnp.exp2(m - m_max) for m in maxes]
    den = w[0] * dens[0] + w[1] * dens[1] + w[2] * dens[2]
    acc = [None] * nchunk
    for g in range(N_GROUPS):
        alpha = w[g] / den
        hi = alpha.astype(BF16)
        lo = (alpha - hi.astype(F32)).astype(BF16)
        wide = _dot(jnp.concatenate([hi, lo], axis=1), expand_ref[...])
        for c in range(nchunk):
            term = wide[:, c * LANES:(c + 1) * LANES] * nums[g][c]
            acc[c] = term if g == 0 else acc[c] + term
    for c in range(nchunk):
        a_ref[:, c * LANES:(c + 1) * LANES] = acc[c].astype(BF16)
    out_ref[...] = _dot(a_ref[...], w_ref[...])


def _attn_combine_kernel(q_ref, kp_ref, k_ref, kn_ref, vp_ref, v_ref, vn_ref, bias_ref,
                         o1_ref, o2_ref, m1_ref, m2_ref, d1_ref, d2_ref, expand_ref, w_ref,
                         out_ref, kbuf_ref, vbuf_ref, o0_ref, m0_ref, d0_ref,
                         on_ref, mn_ref, dn_ref, tmp_ref, a_ref, *, phase_len):
    tile = q_ref.shape[1]
    for buf, parts in ((kbuf_ref, (kp_ref, k_ref, kn_ref)), (vbuf_ref, (vp_ref, v_ref, vn_ref))):
        buf[0, 0:HALF_KEYS, :] = parts[0][0]
        buf[0, HALF_KEYS:HALF_KEYS + tile, :] = parts[1][0]
        buf[0, HALF_KEYS + tile:, :] = parts[2][0]
    _attn_kernel(q_ref, kbuf_ref, vbuf_ref, bias_ref, o0_ref, m0_ref, d0_ref,
                 phase_len=phase_len, key_origin=pl.program_id(2) * tile - HALF_KEYS)
    _combine_kernel(o0_ref, o1_ref, o2_ref, m0_ref, m1_ref, m2_ref, d0_ref, d1_ref, d2_ref,
                    expand_ref, w_ref, out_ref, on_ref, mn_ref, dn_ref, tmp_ref, a_ref)


def _params(n_axes=2):
    return pltpu.CompilerParams(
        dimension_semantics=("arbitrary",) * n_axes, vmem_limit_bytes=VMEM_LIMIT)


def _mixer(x, nw, w_in, cw, w_out, cast=()):
    b, s, d = x.shape
    grid = (b, s // ROW_TILE)
    main, prev, nxt = _seq_specs(s, ROW_TILE)
    c_ops, c_in, c_out, c_shapes = _cast_plan(cast, grid)
    return pl.pallas_call(
        functools.partial(_mixer_kernel, n_cast=len(cast)),
        grid=grid,
        in_specs=[main, prev, nxt, _resident((1, d)), _resident((d, 3 * d)),
                  _resident((3, d)), _resident((d, d))] + c_in,
        out_specs=[main] + c_out,
        out_shape=[jax.ShapeDtypeStruct(x.shape, F32)] + c_shapes,
        scratch_shapes=[pltpu.VMEM((ROW_TILE + 2 * HALO, d), BF16),
                        pltpu.VMEM((ROW_TILE, d), BF16)],
        compiler_params=_params(),
        name="short_conv_mixer",
    )(x, x, x, nw, w_in, cw, w_out, *c_ops)


def _ffn(x, branch, nw, w_up, cw, cb, w_down, fw, final_norm, cast=()):
    b, s, d = x.shape
    f = D_FF
    grid = (b, s // ROW_TILE)
    seq = list(_seq_specs(s, ROW_TILE))
    has_branch = branch is not None
    acts = [x, x, x] + ([branch, branch, branch] if has_branch else [])
    c_ops, c_in, c_out, c_shapes = _cast_plan(cast, grid)
    return pl.pallas_call(
        functools.partial(_ffn_kernel, has_branch=has_branch, final_norm=final_norm,
                          n_cast=len(cast)),
        grid=grid,
        in_specs=seq * (2 if has_branch else 1) + [
            _resident((1, d)), _resident((d, 2 * f)), _resident((3, f)), _resident((1, f)),
            _resident((f, d)), _resident((1, d))] + c_in,
        out_specs=[seq[0]] + c_out,
        out_shape=[jax.ShapeDtypeStruct(x.shape, F32)] + c_shapes,
        scratch_shapes=[pltpu.VMEM((ROW_TILE + 2 * HALO, d), BF16),
                        pltpu.VMEM((ROW_TILE, f), BF16)],
        compiler_params=_params(),
        name="conv_ffn_final" if final_norm else "conv_ffn",
    )(*acts, nw, w_up, cw, cb, w_down, fw, *c_ops)


def _qkv(x, nw, w):
    b, s, d = x.shape
    gw3 = 3 * GROUP_WIDTH
    out_shapes, out_specs = [], []
    for r in DILATIONS:
        out_shapes.append(jax.ShapeDtypeStruct((b, r, s // r, gw3), BF16))
        out_specs.append(pl.BlockSpec((None, r, QKV_TILE // r, gw3),
                                      lambda bi, i: (bi, 0, i, 0)))
    return pl.pallas_call(
        _qkv_kernel,
        grid=(b, s // QKV_TILE),
        in_specs=[pl.BlockSpec((None, QKV_TILE, d), lambda bi, i: (bi, i, 0)),
                  _resident((1, d)), _resident((d, N_GROUPS * gw3))],
        out_specs=out_specs,
        out_shape=out_shapes,
        scratch_shapes=[pltpu.VMEM((d // LANES, QKV_TILE, LANES), F32),
                        pltpu.VMEM((N_GROUPS, QKV_TILE, d), BF16)],
        compiler_params=_params(),
        name="qkv_proj",
    )(x, nw, w)


def _attention(qkv, bias, r):
    b, _, phase_len, _ = qkv.shape
    gw = GROUP_WIDTH
    n_rows = min(ATTN_ROWS, phase_len)
    n_phase = ATTN_ROWS // n_rows
    q_spec = pl.BlockSpec((None, n_phase, n_rows, gw), lambda bi, p, t: (bi, p, t, 0))
    k_spec = pl.BlockSpec((None, n_phase, phase_len, gw), lambda bi, p, t: (bi, p, 0, 1))
    v_spec = pl.BlockSpec((None, n_phase, phase_len, gw), lambda bi, p, t: (bi, p, 0, 2))
    stat_spec = pl.BlockSpec((None, n_phase, n_rows, LANES), lambda bi, p, t: (bi, p, t, 0))
    stat_shape = jax.ShapeDtypeStruct((b, r, phase_len, LANES), F32)
    return pl.pallas_call(
        functools.partial(_attn_kernel, phase_len=phase_len),
        grid=(b, r // n_phase, phase_len // n_rows),
        in_specs=[q_spec, k_spec, v_spec, _resident(bias.shape)],
        out_specs=[q_spec, stat_spec, stat_spec],
        out_shape=[jax.ShapeDtypeStruct((b, r, phase_len, gw), BF16), stat_shape, stat_shape],
        compiler_params=_params(3),
        name=f"local_attn_d{r}",
    )(qkv, qkv, qkv, bias)


def _attention_combine(qkv0, bias0, outs, maxes, dens, w_out):
    b, _, s, _ = qkv0.shape
    d = D_MODEL
    gw = GROUP_WIDTH
    tile = ROW_TILE
    rest = DILATIONS[1:]
    q_spec = pl.BlockSpec((None, 1, tile, gw), lambda bi, p, t: (bi, 0, t, 0))
    per = tile // HALF_KEYS
    last = s // HALF_KEYS - 1
    kv_specs = []
    for col in (1, 2):
        kv_specs += [
            pl.BlockSpec((None, 1, HALF_KEYS, gw),
                         lambda bi, p, t, col=col: (bi, 0, jnp.maximum(t * per - 1, 0), col)),
            pl.BlockSpec((None, 1, tile, gw), lambda bi, p, t, col=col: (bi, 0, t, col)),
            pl.BlockSpec((None, 1, HALF_KEYS, gw),
                         lambda bi, p, t, col=col: (bi, 0, jnp.minimum((t + 1) * per, last), col))]
    o_specs = [pl.BlockSpec((None, r, tile // r, gw), lambda bi, p, t: (bi, 0, t, 0))
               for r in rest]
    stat_specs = [pl.BlockSpec((None, r, tile // r, LANES), lambda bi, p, t: (bi, 0, t, 0))
                  for r in rest]
    expand = np.zeros((2 * LANES, gw), np.float32)
    for h in range(HEADS_PER_GROUP):
        expand[h, h * HEAD_DIM:(h + 1) * HEAD_DIM] = 1.0
        expand[LANES + h, h * HEAD_DIM:(h + 1) * HEAD_DIM] = 1.0
    return pl.pallas_call(
        functools.partial(_attn_combine_kernel, phase_len=s),
        grid=(b, 1, s // tile),
        in_specs=[q_spec] + kv_specs + [_resident(bias0.shape)] + o_specs + stat_specs
        + stat_specs + [_resident((2 * LANES, gw)), _resident((gw, d))],
        out_specs=pl.BlockSpec((None, tile, d), lambda bi, p, t: (bi, t, 0)),
        out_shape=jax.ShapeDtypeStruct((b, s, d), F32),
        scratch_shapes=[pltpu.VMEM((1, tile + 2 * HALF_KEYS, gw), BF16),
                        pltpu.VMEM((1, tile + 2 * HALF_KEYS, gw), BF16),
                        pltpu.VMEM((1, tile, gw), BF16),
                        pltpu.VMEM((1, tile, LANES), F32),
                        pltpu.VMEM((1, tile, LANES), F32),
                        pltpu.VMEM((N_GROUPS - 1, gw // LANES, tile, LANES), F32),
                        pltpu.VMEM((N_GROUPS - 1, tile, LANES), F32),
                        pltpu.VMEM((N_GROUPS - 1, tile, LANES), F32),
                        pltpu.VMEM((gw // LANES + 2, tile, LANES), F32),
                        pltpu.VMEM((tile, gw), BF16)],
        compiler_params=_params(3),
        name="local_attn_d1_combine_out",
    )(*([qkv0] * 7), bias0, *outs, *maxes, *dens, jnp.asarray(expand, BF16), w_out)


def _t5_bucket(rel):
    half = N_BUCKETS // 2
    max_exact = half // 2
    n = jnp.abs(rel)
    side = jnp.where(rel > 0, half, 0)
    nf = jnp.maximum(n, 1).astype(jnp.float32)
    large = max_exact + (jnp.log(nf / max_exact) / math.log(MAX_DISTANCE / max_exact)
                         * (half - max_exact)).astype(jnp.int32)
    large = jnp.minimum(large, half - 1)
    return side + jnp.where(n < max_exact, n, large)


def _bias_tiles(rel_bias, g):
    r = DILATIONS[g]
    heads = HEADS_PER_GROUP
    offs = r * jnp.arange(-HALF_KEYS, HALF_KEYS + 1, dtype=jnp.int32)
    onehot = _t5_bucket(offs)[:, None] == jnp.arange(N_BUCKETS)[None, :]
    table = rel_bias[:, g * heads:(g + 1) * heads].astype(F32)
    per_offset = jnp.sum(jnp.where(onehot[:, :, None], table[None], 0.0), axis=1) * LOG2E
    n = 4 * Q_TILE
    row = jnp.full((heads, n + 1), NEG_INF, F32)
    row = row.at[:, 2 * Q_TILE - HALF_KEYS:2 * Q_TILE + HALF_KEYS + 1].set(per_offset.T)
    skew = jnp.tile(row, (1, Q_TILE))[:, :Q_TILE * n].reshape(heads, Q_TILE, n)
    tiles = jnp.stack([skew[:, :, 2 * Q_TILE - HALF_KEYS * v:2 * Q_TILE - HALF_KEYS * v + K_TILE]
                       for v in range(3)])
    return tiles.reshape(3, heads // 2, 2 * Q_TILE, K_TILE)


def kernel(x, norm_w, conv_in, conv_w, conv_out, attn_qkv, attn_out, rel_bias,
           ffn_up, ffn_conv_w, ffn_conv_b, ffn_down, final_norm):
    d = D_MODEL
    fw = final_norm.reshape(1, d)

    x, w_up0, w_down0 = _mixer(
        x, norm_w[0, 0].reshape(1, d), conv_in[0].astype(BF16), conv_w[0],
        conv_out[0].astype(BF16), cast=((ffn_up, 0), (ffn_down, 0)))
    x, w_qkv, w_attn_out, w_up1, w_down1 = _ffn(
        x, None, norm_w[0, 1].reshape(1, d), w_up0, ffn_conv_w[0],
        ffn_conv_b[0].reshape(1, D_FF), w_down0, fw, False,
        cast=((attn_qkv, 0), (attn_out, 0), (ffn_up, 1), (ffn_down, 1)))

    qkvs = _qkv(x, norm_w[1, 0].reshape(1, d), w_qkv)
    outs, maxes, dens = [], [], []
    for g in range(1, N_GROUPS):
        o, m, den = _attention(qkvs[g], _bias_tiles(rel_bias, g), DILATIONS[g])
        outs.append(o)
        maxes.append(m)
        dens.append(den)
    branch = _attention_combine(qkvs[0], _bias_tiles(rel_bias, 0), outs, maxes, dens, w_attn_out)
    (x,) = _ffn(x, branch, norm_w[1, 1].reshape(1, d), w_up1, ffn_conv_w[1],
                ffn_conv_b[1].reshape(1, D_FF), w_down1, fw, True)
    return x
```

```python
import functools
import math

import jax
import jax.numpy as jnp
import numpy as np
from jax import lax
from jax.experimental import pallas as pl
from jax.experimental.pallas import tpu as pltpu

D_MODEL = 1024
D_FF = 2816
EPS = 1e-6
DILATIONS = (1, 4, 16)
N_GROUPS = 3
HEADS_PER_GROUP = 8
HEAD_DIM = 64
GROUP_WIDTH = HEADS_PER_GROUP * HEAD_DIM
N_BUCKETS = 32
MAX_DISTANCE = 1024
NEG_INF = -1e30

HALF_KEYS = 64
LANES = 128
HALO = 16
ROW_TILE = 1024
QKV_TILE = 1024
COL_CHUNK = 256
DOWN_ROW_BLOCKS = 4
MAX_STRIDE = 4
Q_TILE = 128
K_TILE = Q_TILE + 2 * HALF_KEYS
ATTN_ROWS = 2048
LOG2E = math.log2(math.e)
V7X_VMEM_BYTES = 64 * 1024 * 1024
VMEM_LIMIT = V7X_VMEM_BYTES - 8 * 1024 * 1024

BF16 = jnp.bfloat16
F32 = jnp.float32


def _dot(a, b):
    return jnp.dot(a, b, preferred_element_type=F32)


def _rms(x, w):
    return x * lax.rsqrt(jnp.mean(x * x, axis=-1, keepdims=True) + EPS) * w


def _resident(shape):
    zeros = (0,) * len(shape)
    return pl.BlockSpec(shape, lambda *_: zeros, pipeline_mode=pl.Buffered(1))


def _seq_specs(seq, tile):
    per = tile // HALO
    last = seq // HALO - 1
    main = pl.BlockSpec((None, tile, D_MODEL), lambda b, i: (b, i, 0))
    prev = pl.BlockSpec((None, HALO, D_MODEL),
                        lambda b, i: (b, jnp.maximum(i * per - 1, 0), 0))
    nxt = pl.BlockSpec((None, HALO, D_MODEL),
                       lambda b, i: (b, jnp.minimum((i + 1) * per, last), 0))
    return main, prev, nxt


def _conv3(p, cw_ref, cols, tile):
    i = pl.program_id(1)
    rows = lax.broadcasted_iota(jnp.int32, (p.shape[0], 1), 0)
    outside = ((i == 0) & (rows < HALO)) | (
        (i == pl.num_programs(1) - 1) & (rows >= HALO + tile))
    p = jnp.where(outside, 0.0, p)
    n = p.shape[0]
    before = pltpu.roll(p, 1, axis=0)[HALO:HALO + tile]
    after = pltpu.roll(p, n - 1, axis=0)[HALO:HALO + tile]
    mid = p[HALO:HALO + tile]
    return (cw_ref[0:1, cols] * before + cw_ref[1:2, cols] * mid
            + cw_ref[2:3, cols] * after)


def _cast_plan(weights, grid):
    n_steps = grid[0] * grid[1]
    operands, in_specs, out_specs, out_shapes = [], [], [], []
    for w, layer in weights:
        _, rows, cols = w.shape
        n_blocks = next(n for n in (n_steps, n_steps // 2, n_steps // 4)
                        if rows % n == 0 and (rows // n) % HALO == 0)
        rep = n_steps // n_blocks

        def slab(b, i, rep=rep):
            return (b * grid[1] + i) // rep

        operands.append(w)
        in_specs.append(pl.BlockSpec((None, rows // n_blocks, cols),
                                     lambda b, i, layer=layer, slab=slab: (layer, slab(b, i), 0)))
        out_specs.append(pl.BlockSpec((rows // n_blocks, cols),
                                      lambda b, i, slab=slab: (slab(b, i), 0)))
        out_shapes.append(jax.ShapeDtypeStruct((rows, cols), BF16))
    return operands, in_specs, out_specs, out_shapes


def _cast_slabs(src_refs, dst_refs):
    for src, dst in zip(src_refs, dst_refs):
        dst[...] = src[...].astype(BF16)


def _mixer_kernel(*refs, n_cast):
    x_ref, xp_ref, xn_ref, nw_ref, win_ref, cw_ref, wout_ref = refs[:7]
    cast_src = refs[7:7 + n_cast]
    o_ref = refs[7 + n_cast]
    cast_dst = refs[8 + n_cast:8 + 2 * n_cast]
    hn_ref, a_ref = refs[8 + 2 * n_cast:]
    _cast_slabs(cast_src, cast_dst)
    tile = x_ref.shape[0]
    d = D_MODEL
    nw = nw_ref[...]
    hn_ref[0:HALO, :] = _rms(xp_ref[...], nw).astype(BF16)
    hn_ref[HALO:HALO + tile, :] = _rms(x_ref[...], nw).astype(BF16)
    hn_ref[HALO + tile:, :] = _rms(xn_ref[...], nw).astype(BF16)
    for j in range(d // COL_CHUNK):
        cols = slice(j * COL_CHUNK, (j + 1) * COL_CHUNK)
        gate = _dot(hn_ref[HALO:HALO + tile, :], win_ref[:, cols])
        c = _dot(hn_ref[...], win_ref[:, d + j * COL_CHUNK:d + (j + 1) * COL_CHUNK])
        h = _dot(hn_ref[...], win_ref[:, 2 * d + j * COL_CHUNK:2 * d + (j + 1) * COL_CHUNK])
        conv = _conv3(c * h, cw_ref, cols, tile)
        a_ref[:, cols] = (gate * conv).astype(BF16)
    o_ref[...] = x_ref[...] + _dot(a_ref[...], wout_ref[...])


def _ffn_kernel(*refs, has_branch, final_norm, n_cast):
    if has_branch:
        x_ref, xp_ref, xn_ref, y_ref, yp_ref, yn_ref = refs[:6]
        refs = refs[6:]
        tiles = [xp_ref[...] + yp_ref[...], x_ref[...] + y_ref[...], xn_ref[...] + yn_ref[...]]
    else:
        x_ref, xp_ref, xn_ref = refs[:3]
        refs = refs[3:]
        tiles = [xp_ref[...], x_ref[...], xn_ref[...]]
    nw_ref, wup_ref, cw_ref, cb_ref, wdown_ref, fw_ref = refs[:6]
    cast_src = refs[6:6 + n_cast]
    o_ref = refs[6 + n_cast]
    cast_dst = refs[7 + n_cast:7 + 2 * n_cast]
    hn_ref, a_ref = refs[7 + 2 * n_cast:]
    _cast_slabs(cast_src, cast_dst)
    tile = x_ref.shape[0]
    f = D_FF
    nw = nw_ref[...]
    hn_ref[0:HALO, :] = _rms(tiles[0], nw).astype(BF16)
    hn_ref[HALO:HALO + tile, :] = _rms(tiles[1], nw).astype(BF16)
    hn_ref[HALO + tile:, :] = _rms(tiles[2], nw).astype(BF16)
    for j in range(f // COL_CHUNK):
        cols = slice(j * COL_CHUNK, (j + 1) * COL_CHUNK)
        g = _dot(hn_ref[...], wup_ref[:, cols])
        u = _dot(hn_ref[HALO:HALO + tile, :],
                 wup_ref[:, f + j * COL_CHUNK:f + (j + 1) * COL_CHUNK])
        g = _conv3(g, cw_ref, cols, tile) + cb_ref[:, cols]
        act = g / (1.0 + jnp.exp(-g))
        a_ref[:, cols] = (act * u).astype(BF16)
    block = tile // DOWN_ROW_BLOCKS
    for rows in (slice(k * block, (k + 1) * block) for k in range(DOWN_ROW_BLOCKS)):
        y = x_ref[rows, :] + _dot(a_ref[rows, :], wdown_ref[...])
        if has_branch:
            y = y + y_ref[rows, :]
        if final_norm:
            y = _rms(y, fw_ref[...])
        o_ref[rows, :] = y


def _qkv_kernel(x_ref, nw_ref, w_ref, o0_ref, o1_ref, o2_ref, hn_ref, hp_ref):
    tile = x_ref.shape[0]
    gw = GROUP_WIDTH
    hn = _rms(x_ref[...], nw_ref[...])
    for c in range(D_MODEL // LANES):
        hn_ref[c] = hn[:, c * LANES:(c + 1) * LANES]
    for g, (r, o_ref) in enumerate(zip(DILATIONS, (o0_ref, o1_ref, o2_ref))):
        rows = tile // r
        for p in range(r):
            for c in range(D_MODEL // LANES):
                hp_ref[g, p * rows:(p + 1) * rows, c * LANES:(c + 1) * LANES] = (
                    hn_ref[c, pl.ds(p, rows, stride=r), :].astype(BF16))
        for which in range(3):
            col = (which * N_GROUPS + g) * gw
            res = _dot(hp_ref[g], w_ref[:, col:col + gw])
            if which == 0:
                res = res * (HEAD_DIM ** -0.5 * LOG2E)
            for p in range(r):
                o_ref[p, :, which * gw:(which + 1) * gw] = (
                    res[p * rows:(p + 1) * rows].astype(BF16))


def _attn_kernel(q_ref, k_ref, v_ref, bias_ref, o_ref, m_ref, d_ref, *, phase_len,
                 key_origin=0):
    n_phase, n_rows, _ = q_ref.shape
    t = pl.program_id(2)
    lane = lax.broadcasted_iota(jnp.int32, (1, LANES), 1)
    low = lane < HEAD_DIM
    for ph in range(n_phase):
        for st in range(n_rows // Q_TILE):
            q0 = t * n_rows + st * Q_TILE
            first_key = jnp.clip(q0 - HALF_KEYS, 0, phase_len - K_TILE)
            variant = lax.shift_right_logical(q0 - first_key, HALF_KEYS.bit_length() - 1)
            start = pl.multiple_of(first_key - key_origin, HALF_KEYS)
            rows = slice(st * Q_TILE, (st + 1) * Q_TILE)
            m_all = jnp.zeros((Q_TILE, LANES), F32)
            d_all = jnp.ones((Q_TILE, LANES), F32)
            for hp in range(HEADS_PER_GROUP // 2):
                cols = slice(hp * LANES, (hp + 1) * LANES)
                q2 = q_ref[ph, rows, cols]
                k2 = k_ref[ph, pl.ds(start, K_TILE), cols]
                v2 = v_ref[ph, pl.ds(start, K_TILE), cols]
                zero = jnp.zeros_like(q2)
                qs = jnp.concatenate([jnp.where(low, q2, zero), jnp.where(low, zero, q2)], axis=0)
                s = lax.dot_general(qs, k2, (((1,), (1,)), ((), ())),
                                    preferred_element_type=F32)
                s = s + bias_ref[variant, hp]
                m = jnp.max(s, axis=-1, keepdims=True)
                e = jnp.exp2(s - m).astype(BF16)
                pv = _dot(e, jnp.concatenate([v2, jnp.ones_like(v2)], axis=1))
                num = pv[:, :LANES]
                den = pv[:, LANES:]
                o_ref[ph, rows, cols] = jnp.where(low, num[:Q_TILE], num[Q_TILE:]).astype(BF16)
                for half in range(2):
                    sel = lane == 2 * hp + half
                    part = slice(half * Q_TILE, (half + 1) * Q_TILE)
                    m_all = jnp.where(sel, m[part], m_all)
                    d_all = jnp.where(sel, den[part], d_all)
            m_ref[ph, rows, :] = m_all
            d_ref[ph, rows, :] = d_all


def _interleave(dst, src_of_phase, r, tile, tmp):
    rows = tile // r
    s = MAX_STRIDE
    if r <= s:
        for p in range(r):
            dst[pl.ds(p, rows, stride=r), :] = src_of_phase(p)
        return
    assert r // s <= s
    for p in range(r):
        hi, lo = divmod(p, s)
        tmp[pl.ds(lo * (tile // s) + hi, rows, stride=r // s), :] = src_of_phase(p)
    for lo in range(s):
        dst[pl.ds(lo, tile // s, stride=s), :] = tmp[lo * (tile // s):(lo + 1) * (tile // s), :]


def _combine_kernel(o0_ref, o1_ref, o2_ref, m0_ref, m1_ref, m2_ref,
                    d0_ref, d1_ref, d2_ref, expand_ref, w_ref,
                    out_ref, on_ref, mn_ref, dn_ref, tmp_ref, a_ref):
    tile = out_ref.shape[0]
    nchunk = GROUP_WIDTH // LANES
    groups = zip(DILATIONS, (o0_ref, o1_ref, o2_ref), (m0_ref, m1_ref, m2_ref),
                 (d0_ref, d1_ref, d2_ref))
    nums, maxes, dens = [], [], []
    for g, (r, o_ref, m_ref, d_ref) in enumerate(groups):
        if r == 1:
            nums.append([o_ref[0, :, c * LANES:(c + 1) * LANES].astype(F32)
                         for c in range(nchunk)])
            maxes.append(m_ref[0])
            dens.append(d_ref[0])
            continue
        for c in range(nchunk):
            _interleave(on_ref.at[g - 1, c],
                        lambda p, c=c: o_ref[p, :, c * LANES:(c + 1) * LANES].astype(F32),
                        r, tile, tmp_ref.at[c])
        _interleave(mn_ref.at[g - 1], lambda p: m_ref[p], r, tile, tmp_ref.at[nchunk])
        _interleave(dn_ref.at[g - 1], lambda p: d_ref[p], r, tile, tmp_ref.at[nchunk + 1])
        nums.append([on_ref[g - 1, c] for c in range(nchunk)])
        maxes.append(mn_ref[g - 1])
        dens.append(dn_ref[g - 1])
    m_max = jnp.maximum(jnp.maximum(maxes[0], maxes[1]), maxes[2])
    w = [jnp.exp2(m - m_max) for m in maxes]
    den = w[0] * dens[0] + w[1] * dens[1] + w[2] * dens[2]
    acc = [None] * nchunk
    for g in range(N_GROUPS):
        alpha = w[g] / den
        hi = alpha.astype(BF16)
        lo = (alpha - hi.astype(F32)).astype(BF16)
        wide = _dot(jnp.concatenate([hi, lo], axis=1), expand_ref[...])
        for c in range(nchunk):
            term = wide[:, c * LANES:(c + 1) * LANES] * nums[g][c]
            acc[c] = term if g == 0 else acc[c] + term
    for c in range(nchunk):
        a_ref[:, c * LANES:(c + 1) * LANES] = acc[c].astype(BF16)
    out_ref[...] = _dot(a_ref[...], w_ref[...])


def _attn_combine_kernel(q_ref, kp_ref, k_ref, kn_ref, vp_ref, v_ref, vn_ref, bias_ref,
                         o1_ref, o2_ref, m1_ref, m2_ref, d1_ref, d2_ref, expand_ref, w_ref,
                         out_ref, kbuf_ref, vbuf_ref, o0_ref, m0_ref, d0_ref,
                         on_ref, mn_ref, dn_ref, tmp_ref, a_ref, *, phase_len):
    tile = q_ref.shape[1]
    for buf, parts in ((kbuf_ref, (kp_ref, k_ref, kn_ref)), (vbuf_ref, (vp_ref, v_ref, vn_ref))):
        buf[0, 0:HALF_KEYS, :] = parts[0][0]
        buf[0, HALF_KEYS:HALF_KEYS + tile, :] = parts[1][0]
        buf[0, HALF_KEYS + tile:, :] = parts[2][0]
    _attn_kernel(q_ref, kbuf_ref, vbuf_ref, bias_ref, o0_ref, m0_ref, d0_ref,
                 phase_len=phase_len, key_origin=pl.program_id(2) * tile - HALF_KEYS)
    _combine_kernel(o0_ref, o1_ref, o2_ref, m0_ref, m1_ref, m2_ref, d0_ref, d1_ref, d2_ref,
                    expand_ref, w_ref, out_ref, on_ref, mn_ref, dn_ref, tmp_ref, a_ref)


def _params(n_axes=2):
    return pltpu.CompilerParams(
        dimension_semantics=("arbitrary",) * n_axes, vmem_limit_bytes=VMEM_LIMIT)


def _mixer(x, nw, w_in, cw, w_out, cast=()):
    b, s, d = x.shape
    grid = (b, s // ROW_TILE)
    main, prev, nxt = _seq_specs(s, ROW_TILE)
    c_ops, c_in, c_out, c_shapes = _cast_plan(cast, grid)
    return pl.pallas_call(
        functools.partial(_mixer_kernel, n_cast=len(cast)),
        grid=grid,
        in_specs=[main, prev, nxt, _resident((1, d)), _resident((d, 3 * d)),
                  _resident((3, d)), _resident((d, d))] + c_in,
        out_specs=[main] + c_out,
        out_shape=[jax.ShapeDtypeStruct(x.shape, F32)] + c_shapes,
        scratch_shapes=[pltpu.VMEM((ROW_TILE + 2 * HALO, d), BF16),
                        pltpu.VMEM((ROW_TILE, d), BF16)],
        compiler_params=_params(),
        name="short_conv_mixer",
    )(x, x, x, nw, w_in, cw, w_out, *c_ops)


def _ffn(x, branch, nw, w_up, cw, cb, w_down, fw, final_norm, cast=()):
    b, s, d = x.shape
    f = D_FF
    grid = (b, s // ROW_TILE)
    seq = list(_seq_specs(s, ROW_TILE))
    has_branch = branch is not None
    acts = [x, x, x] + ([branch, branch, branch] if has_branch else [])
    c_ops, c_in, c_out, c_shapes = _cast_plan(cast, grid)
    return pl.pallas_call(
        functools.partial(_ffn_kernel, has_branch=has_branch, final_norm=final_norm,
                          n_cast=len(cast)),
        grid=grid,
        in_specs=seq * (2 if has_branch else 1) + [
            _resident((1, d)), _resident((d, 2 * f)), _resident((3, f)), _resident((1, f)),
            _resident((f, d)), _resident((1, d))] + c_in,
        out_specs=[seq[0]] + c_out,
        out_shape=[jax.ShapeDtypeStruct(x.shape, F32)] + c_shapes,
        scratch_shapes=[pltpu.VMEM((ROW_TILE + 2 * HALO, d), BF16),
                        pltpu.VMEM((ROW_TILE, f), BF16)],
        compiler_params=_params(),
        name="conv_ffn_final" if final_norm else "conv_ffn",
    )(*acts, nw, w_up, cw, cb, w_down, fw, *c_ops)


def _qkv(x, nw, w):
    b, s, d = x.shape
    gw3 = 3 * GROUP_WIDTH
    out_shapes, out_specs = [], []
    for r in DILATIONS:
        out_shapes.append(jax.ShapeDtypeStruct((b, r, s // r, gw3), BF16))
        out_specs.append(pl.BlockSpec((None, r, QKV_TILE // r, gw3),
                                      lambda bi, i: (bi, 0, i, 0)))
    return pl.pallas_call(
        _qkv_kernel,
        grid=(b, s // QKV_TILE),
        in_specs=[pl.BlockSpec((None, QKV_TILE, d), lambda bi, i: (bi, i, 0)),
                  _resident((1, d)), _resident((d, N_GROUPS * gw3))],
        out_specs=out_specs,
        out_shape=out_shapes,
        scratch_shapes=[pltpu.VMEM((d // LANES, QKV_TILE, LANES), F32),
                        pltpu.VMEM((N_GROUPS, QKV_TILE, d), BF16)],
        compiler_params=_params(),
        name="qkv_proj",
    )(x, nw, w)


def _attention(qkv, bias, r):
    b, _, phase_len, _ = qkv.shape
    gw = GROUP_WIDTH
    n_rows = min(ATTN_ROWS, phase_len)
    n_phase = ATTN_ROWS // n_rows
    q_spec = pl.BlockSpec((None, n_phase, n_rows, gw), lambda bi, p, t: (bi, p, t, 0))
    k_spec = pl.BlockSpec((None, n_phase, phase_len, gw), lambda bi, p, t: (bi, p, 0, 1))
    v_spec = pl.BlockSpec((None, n_phase, phase_len, gw), lambda bi, p, t: (bi, p, 0, 2))
    stat_spec = pl.BlockSpec((None, n_phase, n_rows, LANES), lambda bi, p, t: (bi, p, t, 0))
    stat_shape = jax.ShapeDtypeStruct((b, r, phase_len, LANES), F32)
    return pl.pallas_call(
        functools.partial(_attn_kernel, phase_len=phase_len),
        grid=(b, r // n_phase, phase_len // n_rows),
        in_specs=[q_spec, k_spec, v_spec, _resident(bias.shape)],
        out_specs=[q_spec, stat_spec, stat_spec],
        out_shape=[jax.ShapeDtypeStruct((b, r, phase_len, gw), BF16), stat_shape, stat_shape],
        compiler_params=_params(3),
        name=f"local_attn_d{r}",
    )(qkv, qkv, qkv, bias)


def _attention_combine(qkv0, bias0, outs, maxes, dens, w_out):
    b, _, s, _ = qkv0.shape
    d = D_MODEL
    gw = GROUP_WIDTH
    tile = ROW_TILE
    rest = DILATIONS[1:]
    q_spec = pl.BlockSpec((None, 1, tile, gw), lambda bi, p, t: (bi, 0, t, 0))
    per = tile // HALF_KEYS
    last = s // HALF_KEYS - 1
    kv_specs = []
    for col in (1, 2):
        kv_specs += [
            pl.BlockSpec((None, 1, HALF_KEYS, gw),
                         lambda bi, p, t, col=col: (bi, 0, jnp.maximum(t * per - 1, 0), col)),
            pl.BlockSpec((None, 1, tile, gw), lambda bi, p, t, col=col: (bi, 0, t, col)),
            pl.BlockSpec((None, 1, HALF_KEYS, gw),
                         lambda bi, p, t, col=col: (bi, 0, jnp.minimum((t + 1) * per, last), col))]
    o_specs = [pl.BlockSpec((None, r, tile // r, gw), lambda bi, p, t: (bi, 0, t, 0))
               for r in rest]
    stat_specs = [pl.BlockSpec((None, r, tile // r, LANES), lambda bi, p, t: (bi, 0, t, 0))
                  for r in rest]
    expand = np.zeros((2 * LANES, gw), np.float32)
    for h in range(HEADS_PER_GROUP):
        expand[h, h * HEAD_DIM:(h + 1) * HEAD_DIM] = 1.0
        expand[LANES + h, h * HEAD_DIM:(h + 1) * HEAD_DIM] = 1.0
    return pl.pallas_call(
        functools.partial(_attn_combine_kernel, phase_len=s),
        grid=(b, 1, s // tile),
        in_specs=[q_spec] + kv_specs + [_resident(bias0.shape)] + o_specs + stat_specs
        + stat_specs + [_resident((2 * LANES, gw)), _resident((gw, d))],
        out_specs=pl.BlockSpec((None, tile, d), lambda bi, p, t: (bi, t, 0)),
        out_shape=jax.ShapeDtypeStruct((b, s, d), F32),
        scratch_shapes=[pltpu.VMEM((1, tile + 2 * HALF_KEYS, gw), BF16),
                        pltpu.VMEM((1, tile + 2 * HALF_KEYS, gw), BF16),
                        pltpu.VMEM((1, tile, gw), BF16),
                        pltpu.VMEM((1, tile, LANES), F32),
                        pltpu.VMEM((1, tile, LANES), F32),
                        pltpu.VMEM((N_GROUPS - 1, gw // LANES, tile, LANES), F32),
                        pltpu.VMEM((N_GROUPS - 1, tile, LANES), F32),
                        pltpu.VMEM((N_GROUPS - 1, tile, LANES), F32),
                        pltpu.VMEM((gw // LANES + 2, tile, LANES), F32),
                        pltpu.VMEM((tile, gw), BF16)],
        compiler_params=_params(3),
        name="local_attn_d1_combine_out",
    )(*([qkv0] * 7), bias0, *outs, *maxes, *dens, jnp.asarray(expand, BF16), w_out)


def _t5_bucket(rel):
    half = N_BUCKETS // 2
    max_exact = half // 2
    n = jnp.abs(rel)
    side = jnp.where(rel > 0, half, 0)
    nf = jnp.maximum(n, 1).astype(jnp.float32)
    large = max_exact + (jnp.log(nf / max_exact) / math.log(MAX_DISTANCE / max_exact)
                         * (half - max_exact)).astype(jnp.int32)
    large = jnp.minimum(large, half - 1)
    return side + jnp.where(n < max_exact, n, large)


def _bias_tiles(rel_bias, g):
    r = DILATIONS[g]
    heads = HEADS_PER_GROUP
    offs = r * jnp.arange(-HALF_KEYS, HALF_KEYS + 1, dtype=jnp.int32)
    onehot = _t5_bucket(offs)[:, None] == jnp.arange(N_BUCKETS)[None, :]
    table = rel_bias[:, g * heads:(g + 1) * heads].astype(F32)
    per_offset = jnp.sum(jnp.where(onehot[:, :, None], table[None], 0.0), axis=1) * LOG2E
    n = 4 * Q_TILE
    row = jnp.full((heads, n + 1), NEG_INF, F32)
    row = row.at[:, 2 * Q_TILE - HALF_KEYS:2 * Q_TILE + HALF_KEYS + 1].set(per_offset.T)
    skew = jnp.tile(row, (1, Q_TILE))[:, :Q_TILE * n].reshape(heads, Q_TILE, n)
    tiles = jnp.stack([skew[:, :, 2 * Q_TILE - HALF_KEYS * v:2 * Q_TILE - HALF_KEYS * v + K_TILE]
                       for v in range(3)])
    return tiles.reshape(3, heads // 2, 2 * Q_TILE, K_TILE)


def kernel(x, norm_w, conv_in, conv_w, conv_out, attn_qkv, attn_out, rel_bias,
           ffn_up, ffn_conv_w, ffn_conv_b, ffn_down, final_norm):
    d = D_MODEL
    fw = final_norm.reshape(1, d)

    x, w_up0, w_down0 = _mixer(
        x, norm_w[0, 0].reshape(1, d), conv_in[0].astype(BF16), conv_w[0],
        conv_out[0].astype(BF16), cast=((ffn_up, 0), (ffn_down, 0)))
    x, w_qkv, w_attn_out, w_up1, w_down1 = _ffn(
        x, None, norm_w[0, 1].reshape(1, d), w_up0, ffn_conv_w[0],
        ffn_conv_b[0].reshape(1, D_FF), w_down0, fw, False,
        cast=((attn_qkv, 0), (attn_out, 0), (ffn_up, 1), (ffn_down, 1)))

    qkvs = _qkv(x, norm_w[1, 0].reshape(1, d), w_qkv)
    outs, maxes, dens = [], [], []
    for g in range(1, N_GROUPS):
        o, m, den = _attention(qkvs[g], _bias_tiles(rel_bias, g), DILATIONS[g])
        outs.append(o)
        maxes.append(m)
        dens.append(den)
    branch = _attention_combine(qkvs[0], _bias_tiles(rel_bias, 0), outs, maxes, dens, w_attn_out)
    (x,) = _ffn(x, branch, norm_w[1, 1].reshape(1, d), w_up1, ffn_conv_w[1],
                ffn_conv_b[1].reshape(1, D_FF), w_down1, fw, True)
    return x
```

```python
import functools
import math

import jax
import jax.numpy as jnp
import numpy as np
from jax import lax
from jax.experimental import pallas as pl
from jax.experimental.pallas import tpu as pltpu

D_MODEL = 1024
D_FF = 2816
EPS = 1e-6
DILATIONS = (1, 4, 16)
N_GROUPS = 3
HEADS_PER_GROUP = 8
HEAD_DIM = 64
GROUP_WIDTH = HEADS_PER_GROUP * HEAD_DIM
N_BUCKETS = 32
MAX_DISTANCE = 1024
NEG_INF = -1e30

HALF_KEYS = 64
LANES = 128
HALO = 16
ROW_TILE = 1024
QKV_TILE = 1024
COL_CHUNK = 256
DOWN_ROW_BLOCKS = 4
MAX_STRIDE = 4
Q_TILE = 128
K_TILE = Q_TILE + 2 * HALF_KEYS
ATTN_ROWS = 2048
LOG2E = math.log2(math.e)
V7X_VMEM_BYTES = 64 * 1024 * 1024
VMEM_LIMIT = V7X_VMEM_BYTES - 8 * 1024 * 1024

BF16 = jnp.bfloat16
F32 = jnp.float32


def _dot(a, b):
    return jnp.dot(a, b, preferred_element_type=F32)


def _rms(x, w):
    return x * lax.rsqrt(jnp.mean(x * x, axis=-1, keepdims=True) + EPS) * w


def _resident(shape):
    zeros = (0,) * len(shape)
    return pl.BlockSpec(shape, lambda *_: zeros, pipeline_mode=pl.Buffered(1))


def _seq_specs(seq, tile):
    per = tile // HALO
    last = seq // HALO - 1
    main = pl.BlockSpec((None, tile, D_MODEL), lambda b, i: (b, i, 0))
    prev = pl.BlockSpec((None, HALO, D_MODEL),
                        lambda b, i: (b, jnp.maximum(i * per - 1, 0), 0))
    nxt = pl.BlockSpec((None, HALO, D_MODEL),
                       lambda b, i: (b, jnp.minimum((i + 1) * per, last), 0))
    return main, prev, nxt


def _conv3(p, cw_ref, cols, tile):
    i = pl.program_id(1)
    rows = lax.broadcasted_iota(jnp.int32, (p.shape[0], 1), 0)
    outside = ((i == 0) & (rows < HALO)) | (
        (i == pl.num_programs(1) - 1) & (rows >= HALO + tile))
    p = jnp.where(outside, 0.0, p)
    n = p.shape[0]
    before = pltpu.roll(p, 1, axis=0)[HALO:HALO + tile]
    after = pltpu.roll(p, n - 1, axis=0)[HALO:HALO + tile]
    mid = p[HALO:HALO + tile]
    return (cw_ref[0:1, cols] * before + cw_ref[1:2, cols] * mid
            + cw_ref[2:3, cols] * after)


def _cast_plan(weights, grid):
    n_steps = grid[0] * grid[1]
    operands, in_specs, out_specs, out_shapes = [], [], [], []
    for w, layer in weights:
        _, rows, cols = w.shape
        n_blocks = next(n for n in (n_steps, n_steps // 2, n_steps // 4)
                        if rows % n == 0 and (rows // n) % HALO == 0)
        rep = n_steps // n_blocks

        def slab(b, i, rep=rep):
            return (b * grid[1] + i) // rep

        operands.append(w)
        in_specs.append(pl.BlockSpec((None, rows // n_blocks, cols),
                                     lambda b, i, layer=layer, slab=slab: (layer, slab(b, i), 0)))
        out_specs.append(pl.BlockSpec((rows // n_blocks, cols),
                                      lambda b, i, slab=slab: (slab(b, i), 0)))
        out_shapes.append(jax.ShapeDtypeStruct((rows, cols), BF16))
    return operands, in_specs, out_specs, out_shapes


def _cast_slabs(src_refs, dst_refs):
    for src, dst in zip(src_refs, dst_refs):
        dst[...] = src[...].astype(BF16)


def _mixer_kernel(*refs, n_cast):
    x_ref, xp_ref, xn_ref, nw_ref, win_ref, cw_ref, wout_ref = refs[:7]
    cast_src = refs[7:7 + n_cast]
    o_ref = refs[7 + n_cast]
    cast_dst = refs[8 + n_cast:8 + 2 * n_cast]
    hn_ref, a_ref = refs[8 + 2 * n_cast:]
    _cast_slabs(cast_src, cast_dst)
    tile = x_ref.shape[0]
    d = D_MODEL
    nw = nw_ref[...]
    hn_ref[0:HALO, :] = _rms(xp_ref[...], nw).astype(BF16)
    hn_ref[HALO:HALO + tile, :] = _rms(x_ref[...], nw).astype(BF16)
    hn_ref[HALO + tile:, :] = _rms(xn_ref[...], nw).astype(BF16)
    for j in range(d // COL_CHUNK):
        cols = slice(j * COL_CHUNK, (j + 1) * COL_CHUNK)
        gate = _dot(hn_ref[HALO:HALO + tile, :], win_ref[:, cols])
        c = _dot(hn_ref[...], win_ref[:, d + j * COL_CHUNK:d + (j + 1) * COL_CHUNK])
        h = _dot(hn_ref[...], win_ref[:, 2 * d + j * COL_CHUNK:2 * d + (j + 1) * COL_CHUNK])
        conv = _conv3(c * h, cw_ref, cols, tile)
        a_ref[:, cols] = (gate * conv).astype(BF16)
    o_ref[...] = x_ref[...] + _dot(a_ref[...], wout_ref[...])


def _ffn_kernel(*refs, has_branch, final_norm, n_cast):
    if has_branch:
        x_ref, xp_ref, xn_ref, y_ref, yp_ref, yn_ref = refs[:6]
        refs = refs[6:]
        tiles = [xp_ref[...] + yp_ref[...].astype(F32), x_ref[...] + y_ref[...].astype(F32),
                 xn_ref[...] + yn_ref[...].astype(F32)]
    else:
        x_ref, xp_ref, xn_ref = refs[:3]
        refs = refs[3:]
        tiles = [xp_ref[...], x_ref[...], xn_ref[...]]
    nw_ref, wup_ref, cw_ref, cb_ref, wdown_ref, fw_ref = refs[:6]
    cast_src = refs[6:6 + n_cast]
    o_ref = refs[6 + n_cast]
    cast_dst = refs[7 + n_cast:7 + 2 * n_cast]
    hn_ref, a_ref = refs[7 + 2 * n_cast:]
    _cast_slabs(cast_src, cast_dst)
    tile = x_ref.shape[0]
    f = D_FF
    nw = nw_ref[...]
    hn_ref[0:HALO, :] = _rms(tiles[0], nw).astype(BF16)
    hn_ref[HALO:HALO + tile, :] = _rms(tiles[1], nw).astype(BF16)
    hn_ref[HALO + tile:, :] = _rms(tiles[2], nw).astype(BF16)
    for j in range(f // COL_CHUNK):
        cols = slice(j * COL_CHUNK, (j + 1) * COL_CHUNK)
        g = _dot(hn_ref[...], wup_ref[:, cols])
        u = _dot(hn_ref[HALO:HALO + tile, :],
                 wup_ref[:, f + j * COL_CHUNK:f + (j + 1) * COL_CHUNK])
        g = _conv3(g, cw_ref, cols, tile) + cb_ref[:, cols]
        act = g / (1.0 + jnp.exp(-g))
        a_ref[:, cols] = (act * u).astype(BF16)
    block = tile // DOWN_ROW_BLOCKS
    for rows in (slice(k * block, (k + 1) * block) for k in range(DOWN_ROW_BLOCKS)):
        y = x_ref[rows, :] + _dot(a_ref[rows, :], wdown_ref[...])
        if has_branch:
            y = y + y_ref[rows, :].astype(F32)
        if final_norm:
            y = _rms(y, fw_ref[...])
        o_ref[rows, :] = y


def _qkv_kernel(x_ref, nw_ref, w_ref, o0_ref, o1_ref, o2_ref, hn_ref, hp_ref):
    tile = x_ref.shape[0]
    gw = GROUP_WIDTH
    hn = _rms(x_ref[...], nw_ref[...])
    for c in range(D_MODEL // LANES):
        hn_ref[c] = hn[:, c * LANES:(c + 1) * LANES]
    for g, (r, o_ref) in enumerate(zip(DILATIONS, (o0_ref, o1_ref, o2_ref))):
        rows = tile // r
        for p in range(r):
            for c in range(D_MODEL // LANES):
                hp_ref[g, p * rows:(p + 1) * rows, c * LANES:(c + 1) * LANES] = (
                    hn_ref[c, pl.ds(p, rows, stride=r), :].astype(BF16))
        for which in range(3):
            col = (which * N_GROUPS + g) * gw
            res = _dot(hp_ref[g], w_ref[:, col:col + gw])
            if which == 0:
                res = res * (HEAD_DIM ** -0.5 * LOG2E)
            for p in range(r):
                o_ref[p, :, which * gw:(which + 1) * gw] = (
                    res[p * rows:(p + 1) * rows].astype(BF16))


def _attn_kernel(q_ref, k_ref, v_ref, bias_ref, o_ref, m_ref, d_ref, *, phase_len,
                 key_origin=0):
    n_phase, n_rows, _ = q_ref.shape
    t = pl.program_id(2)
    lane = lax.broadcasted_iota(jnp.int32, (1, LANES), 1)
    low = lane < HEAD_DIM
    for ph in range(n_phase):
        for st in range(n_rows // Q_TILE):
            q0 = t * n_rows + st * Q_TILE
            first_key = jnp.clip(q0 - HALF_KEYS, 0, phase_len - K_TILE)
            variant = lax.shift_right_logical(q0 - first_key, HALF_KEYS.bit_length() - 1)
            start = pl.multiple_of(first_key - key_origin, HALF_KEYS)
            rows = slice(st * Q_TILE, (st + 1) * Q_TILE)
            m_all = jnp.zeros((Q_TILE, LANES), F32)
            d_all = jnp.ones((Q_TILE, LANES), F32)
            for hp in range(HEADS_PER_GROUP // 2):
                cols = slice(hp * LANES, (hp + 1) * LANES)
                q2 = q_ref[ph, rows, cols]
                k2 = k_ref[ph, pl.ds(start, K_TILE), cols]
                v2 = v_ref[ph, pl.ds(start, K_TILE), cols]
                zero = jnp.zeros_like(q2)
                qs = jnp.concatenate([jnp.where(low, q2, zero), jnp.where(low, zero, q2)], axis=0)
                s = lax.dot_general(qs, k2, (((1,), (1,)), ((), ())),
                                    preferred_element_type=F32)
                s = s + bias_ref[variant, hp]
                m = jnp.max(s, axis=-1, keepdims=True)
                e = jnp.exp2(s - m).astype(BF16)
                pv = _dot(e, jnp.concatenate([v2, jnp.ones_like(v2)], axis=1))
                num = pv[:, :LANES]
                den = pv[:, LANES:]
                o_ref[ph, rows, cols] = jnp.where(low, num[:Q_TILE], num[Q_TILE:]).astype(BF16)
                for half in range(2):
                    sel = lane == 2 * hp + half
                    part = slice(half * Q_TILE, (half + 1) * Q_TILE)
                    m_all = jnp.where(sel, m[part], m_all)
                    d_all = jnp.where(sel, den[part], d_all)
            m_ref[ph, rows, :] = m_all
            d_ref[ph, rows, :] = d_all


def _interleave(dst, src_of_phase, r, tile, tmp):
    rows = tile // r
    s = MAX_STRIDE
    if r <= s:
        for p in range(r):
            dst[pl.ds(p, rows, stride=r), :] = src_of_phase(p)
        return
    assert r // s <= s
    for p in range(r):
        hi, lo = divmod(p, s)
        tmp[pl.ds(lo * (tile // s) + hi, rows, stride=r // s), :] = src_of_phase(p)
    for lo in range(s):
        dst[pl.ds(lo, tile // s, stride=s), :] = tmp[lo * (tile // s):(lo + 1) * (tile // s), :]


def _combine_kernel(o0_ref, o1_ref, o2_ref, m0_ref, m1_ref, m2_ref,
                    d0_ref, d1_ref, d2_ref, expand_ref, w_ref,
                    out_ref, on_ref, mn_ref, dn_ref, tmp_ref, a_ref):
    tile = out_ref.shape[0]
    nchunk = GROUP_WIDTH // LANES
    groups = zip(DILATIONS, (o0_ref, o1_ref, o2_ref), (m0_ref, m1_ref, m2_ref),
                 (d0_ref, d1_ref, d2_ref))
    nums, maxes, dens = [], [], []
    for g, (r, o_ref, m_ref, d_ref) in enumerate(groups):
        if r == 1:
            nums.append([o_ref[0, :, c * LANES:(c + 1) * LANES].astype(F32)
                         for c in range(nchunk)])
            maxes.append(m_ref[0])
            dens.append(d_ref[0])
            continue
        for c in range(nchunk):
            _interleave(on_ref.at[g - 1, c],
                        lambda p, c=c: o_ref[p, :, c * LANES:(c + 1) * LANES].astype(F32),
                        r, tile, tmp_ref.at[c])
        _interleave(mn_ref.at[g - 1], lambda p: m_ref[p], r, tile, tmp_ref.at[nchunk])
        _interleave(dn_ref.at[g - 1], lambda p: d_ref[p], r, tile, tmp_ref.at[nchunk + 1])
        nums.append([on_ref[g - 1, c] for c in range(nchunk)])
        maxes.append(mn_ref[g - 1])
        dens.append(dn_ref[g - 1])
    m_max = jnp.maximum(jnp.maximum(maxes[0], maxes[1]), maxes[2])
    w = [jnp.exp2(m - m_max) for m in maxes]
    den = w[0] * dens[0] + w[1] * dens[1] + w[2] * dens[2]
    acc = [None] * nchunk
    for g in range(N_GROUPS):
        alpha = w[g] / den
        hi = alpha.astype(BF16)
        lo = (alpha - hi.astype(F32)).astype(BF16)
        wide = _dot(jnp.concatenate([hi, lo], axis=1), expand_ref[...])
        for c in range(nchunk):
            term = wide[:, c * LANES:(c + 1) * LANES] * nums[g][c]
            acc[c] = term if g == 0 else acc[c] + term
    for c in range(nchunk):
        a_ref[:, c * LANES:(c + 1) * LANES] = acc[c].astype(BF16)
    out_ref[...] = _dot(a_ref[...], w_ref[...]).astype(out_ref.dtype)


def _attn_combine_kernel(q_ref, kp_ref, k_ref, kn_ref, vp_ref, v_ref, vn_ref, bias_ref,
                         o1_ref, o2_ref, m1_ref, m2_ref, d1_ref, d2_ref, expand_ref, w_ref,
                         out_ref, kbuf_ref, vbuf_ref, o0_ref, m0_ref, d0_ref,
                         on_ref, mn_ref, dn_ref, tmp_ref, a_ref, *, phase_len):
    tile = q_ref.shape[1]
    for buf, parts in ((kbuf_ref, (kp_ref, k_ref, kn_ref)), (vbuf_ref, (vp_ref, v_ref, vn_ref))):
        buf[0, 0:HALF_KEYS, :] = parts[0][0]
        buf[0, HALF_KEYS:HALF_KEYS + tile, :] = parts[1][0]
        buf[0, HALF_KEYS + tile:, :] = parts[2][0]
    _attn_kernel(q_ref, kbuf_ref, vbuf_ref, bias_ref, o0_ref, m0_ref, d0_ref,
                 phase_len=phase_len, key_origin=pl.program_id(2) * tile - HALF_KEYS)
    _combine_kernel(o0_ref, o1_ref, o2_ref, m0_ref, m1_ref, m2_ref, d0_ref, d1_ref, d2_ref,
                    expand_ref, w_ref, out_ref, on_ref, mn_ref, dn_ref, tmp_ref, a_ref)


def _params(n_axes=2):
    return pltpu.CompilerParams(
        dimension_semantics=("arbitrary",) * n_axes, vmem_limit_bytes=VMEM_LIMIT)


def _mixer(x, nw, w_in, cw, w_out, cast=()):
    b, s, d = x.shape
    grid = (b, s // ROW_TILE)
    main, prev, nxt = _seq_specs(s, ROW_TILE)
    c_ops, c_in, c_out, c_shapes = _cast_plan(cast, grid)
    return pl.pallas_call(
        functools.partial(_mixer_kernel, n_cast=len(cast)),
        grid=grid,
        in_specs=[main, prev, nxt, _resident((1, d)), _resident((d, 3 * d)),
                  _resident((3, d)), _resident((d, d))] + c_in,
        out_specs=[main] + c_out,
        out_shape=[jax.ShapeDtypeStruct(x.shape, F32)] + c_shapes,
        scratch_shapes=[pltpu.VMEM((ROW_TILE + 2 * HALO, d), BF16),
                        pltpu.VMEM((ROW_TILE, d), BF16)],
        compiler_params=_params(),
        name="short_conv_mixer",
    )(x, x, x, nw, w_in, cw, w_out, *c_ops)


def _ffn(x, branch, nw, w_up, cw, cb, w_down, fw, final_norm, cast=()):
    b, s, d = x.shape
    f = D_FF
    grid = (b, s // ROW_TILE)
    seq = list(_seq_specs(s, ROW_TILE))
    has_branch = branch is not None
    acts = [x, x, x] + ([branch, branch, branch] if has_branch else [])
    c_ops, c_in, c_out, c_shapes = _cast_plan(cast, grid)
    return pl.pallas_call(
        functools.partial(_ffn_kernel, has_branch=has_branch, final_norm=final_norm,
                          n_cast=len(cast)),
        grid=grid,
        in_specs=seq * (2 if has_branch else 1) + [
            _resident((1, d)), _resident((d, 2 * f)), _resident((3, f)), _resident((1, f)),
            _resident((f, d)), _resident((1, d))] + c_in,
        out_specs=[seq[0]] + c_out,
        out_shape=[jax.ShapeDtypeStruct(x.shape, F32)] + c_shapes,
        scratch_shapes=[pltpu.VMEM((ROW_TILE + 2 * HALO, d), BF16),
                        pltpu.VMEM((ROW_TILE, f), BF16)],
        compiler_params=_params(),
        name="conv_ffn_final" if final_norm else "conv_ffn",
    )(*acts, nw, w_up, cw, cb, w_down, fw, *c_ops)


def _qkv(x, nw, w):
    b, s, d = x.shape
    gw3 = 3 * GROUP_WIDTH
    out_shapes, out_specs = [], []
    for r in DILATIONS:
        out_shapes.append(jax.ShapeDtypeStruct((b, r, s // r, gw3), BF16))
        out_specs.append(pl.BlockSpec((None, r, QKV_TILE // r, gw3),
                                      lambda bi, i: (bi, 0, i, 0)))
    return pl.pallas_call(
        _qkv_kernel,
        grid=(b, s // QKV_TILE),
        in_specs=[pl.BlockSpec((None, QKV_TILE, d), lambda bi, i: (bi, i, 0)),
                  _resident((1, d)), _resident((d, N_GROUPS * gw3))],
        out_specs=out_specs,
        out_shape=out_shapes,
        scratch_shapes=[pltpu.VMEM((d // LANES, QKV_TILE, LANES), F32),
                        pltpu.VMEM((N_GROUPS, QKV_TILE, d), BF16)],
        compiler_params=_params(),
        name="qkv_proj",
    )(x, nw, w)


def _attention(qkv, bias, r):
    b, _, phase_len, _ = qkv.shape
    gw = GROUP_WIDTH
    n_rows = min(ATTN_ROWS, phase_len)
    n_phase = ATTN_ROWS // n_rows
    q_spec = pl.BlockSpec((None, n_phase, n_rows, gw), lambda bi, p, t: (bi, p, t, 0))
    k_spec = pl.BlockSpec((None, n_phase, phase_len, gw), lambda bi, p, t: (bi, p, 0, 1))
    v_spec = pl.BlockSpec((None, n_phase, phase_len, gw), lambda bi, p, t: (bi, p, 0, 2))
    stat_spec = pl.BlockSpec((None, n_phase, n_rows, LANES), lambda bi, p, t: (bi, p, t, 0))
    stat_shape = jax.ShapeDtypeStruct((b, r, phase_len, LANES), F32)
    return pl.pallas_call(
        functools.partial(_attn_kernel, phase_len=phase_len),
        grid=(b, r // n_phase, phase_len // n_rows),
        in_specs=[q_spec, k_spec, v_spec, _resident(bias.shape)],
        out_specs=[q_spec, stat_spec, stat_spec],
        out_shape=[jax.ShapeDtypeStruct((b, r, phase_len, gw), BF16), stat_shape, stat_shape],
        compiler_params=_params(3),
        name=f"local_attn_d{r}",
    )(qkv, qkv, qkv, bias)


def _attention_combine(qkv0, bias0, outs, maxes, dens, w_out):
    b, _, s, _ = qkv0.shape
    d = D_MODEL
    gw = GROUP_WIDTH
    tile = ROW_TILE
    rest = DILATIONS[1:]
    q_spec = pl.BlockSpec((None, 1, tile, gw), lambda bi, p, t: (bi, 0, t, 0))
    per = tile // HALF_KEYS
    last = s // HALF_KEYS - 1
    kv_specs = []
    for col in (1, 2):
        kv_specs += [
            pl.BlockSpec((None, 1, HALF_KEYS, gw),
                         lambda bi, p, t, col=col: (bi, 0, jnp.maximum(t * per - 1, 0), col)),
            pl.BlockSpec((None, 1, tile, gw), lambda bi, p, t, col=col: (bi, 0, t, col)),
            pl.BlockSpec((None, 1, HALF_KEYS, gw),
                         lambda bi, p, t, col=col: (bi, 0, jnp.minimum((t + 1) * per, last), col))]
    o_specs = [pl.BlockSpec((None, r, tile // r, gw), lambda bi, p, t: (bi, 0, t, 0))
               for r in rest]
    stat_specs = [pl.BlockSpec((None, r, tile // r, LANES), lambda bi, p, t: (bi, 0, t, 0))
                  for r in rest]
    expand = np.zeros((2 * LANES, gw), np.float32)
    for h in range(HEADS_PER_GROUP):
        expand[h, h * HEAD_DIM:(h + 1) * HEAD_DIM] = 1.0
        expand[LANES + h, h * HEAD_DIM:(h + 1) * HEAD_DIM] = 1.0
    return pl.pallas_call(
        functools.partial(_attn_combine_kernel, phase_len=s),
        grid=(b, 1, s // tile),
        in_specs=[q_spec] + kv_specs + [_resident(bias0.shape)] + o_specs + stat_specs
        + stat_specs + [_resident((2 * LANES, gw)), _resident((gw, d))],
        out_specs=pl.BlockSpec((None, tile, d), lambda bi, p, t: (bi, t, 0)),
        out_shape=jax.ShapeDtypeStruct((b, s, d), BF16),
        scratch_shapes=[pltpu.VMEM((1, tile + 2 * HALF_KEYS, gw), BF16),
                        pltpu.VMEM((1, tile + 2 * HALF_KEYS, gw), BF16),
                        pltpu.VMEM((1, tile, gw), BF16),
                        pltpu.VMEM((1, tile, LANES), F32),
                        pltpu.VMEM((1, tile, LANES), F32),
                        pltpu.VMEM((N_GROUPS - 1, gw // LANES, tile, LANES), F32),
                        pltpu.VMEM((N_GROUPS - 1, tile, LANES), F32),
                        pltpu.VMEM((N_GROUPS - 1, tile, LANES), F32),
                        pltpu.VMEM((gw // LANES + 2, tile, LANES), F32),
                        pltpu.VMEM((tile, gw), BF16)],
        compiler_params=_params(3),
        name="local_attn_d1_combine_out",
    )(*([qkv0] * 7), bias0, *outs, *maxes, *dens, jnp.asarray(expand, BF16), w_out)


def _t5_bucket(rel):
    half = N_BUCKETS // 2
    max_exact = half // 2
    n = jnp.abs(rel)
    side = jnp.where(rel > 0, half, 0)
    nf = jnp.maximum(n, 1).astype(jnp.float32)
    large = max_exact + (jnp.log(nf / max_exact) / math.log(MAX_DISTANCE / max_exact)
                         * (half - max_exact)).astype(jnp.int32)
    large = jnp.minimum(large, half - 1)
    return side + jnp.where(n < max_exact, n, large)


def _bias_tiles(rel_bias, g):
    r = DILATIONS[g]
    heads = HEADS_PER_GROUP
    offs = r * jnp.arange(-HALF_KEYS, HALF_KEYS + 1, dtype=jnp.int32)
    onehot = _t5_bucket(offs)[:, None] == jnp.arange(N_BUCKETS)[None, :]
    table = rel_bias[:, g * heads:(g + 1) * heads].astype(F32)
    per_offset = jnp.sum(jnp.where(onehot[:, :, None], table[None], 0.0), axis=1) * LOG2E
    n = 4 * Q_TILE
    row = jnp.full((heads, n + 1), NEG_INF, F32)
    row = row.at[:, 2 * Q_TILE - HALF_KEYS:2 * Q_TILE + HALF_KEYS + 1].set(per_offset.T)
    skew = jnp.tile(row, (1, Q_TILE))[:, :Q_TILE * n].reshape(heads, Q_TILE, n)
    tiles = jnp.stack([skew[:, :, 2 * Q_TILE - HALF_KEYS * v:2 * Q_TILE - HALF_KEYS * v + K_TILE]
                       for v in range(3)])
    return tiles.reshape(3, heads // 2, 2 * Q_TILE, K_TILE)


def kernel(x, norm_w, conv_in, conv_w, conv_out, attn_qkv, attn_out, rel_bias,
           ffn_up, ffn_conv_w, ffn_conv_b, ffn_down, final_norm):
    d = D_MODEL
    fw = final_norm.reshape(1, d)

    x, w_up0, w_down0 = _mixer(
        x, norm_w[0, 0].reshape(1, d), conv_in[0].astype(BF16), conv_w[0],
        conv_out[0].astype(BF16), cast=((ffn_up, 0), (ffn_down, 0)))
    x, w_qkv, w_attn_out, w_up1, w_down1 = _ffn(
        x, None, norm_w[0, 1].reshape(1, d), w_up0, ffn_conv_w[0],
        ffn_conv_b[0].reshape(1, D_FF), w_down0, fw, False,
        cast=((attn_qkv, 0), (attn_out, 0), (ffn_up, 1), (ffn_down, 1)))

    qkvs = _qkv(x, norm_w[1, 0].reshape(1, d), w_qkv)
    outs, maxes, dens = [], [], []
    for g in range(1, N_GROUPS):
        o, m, den = _attention(qkvs[g], _bias_tiles(rel_bias, g), DILATIONS[g])
        outs.append(o)
        maxes.append(m)
        dens.append(den)
    branch = _attention_combine(qkvs[0], _bias_tiles(rel_bias, 0), outs, maxes, dens, w_attn_out)
    (x,) = _ffn(x, branch, norm_w[1, 1].reshape(1, d), w_up1, ffn_conv_w[1],
                ffn_conv_b[1].reshape(1, D_FF), w_down1, fw, True)
    return x
```

```python
import functools
import math

import jax
import jax.numpy as jnp
import numpy as np
from jax import lax
from jax.experimental import pallas as pl
from jax.experimental.pallas import tpu as pltpu

D_MODEL = 1024
D_FF = 2816
EPS = 1e-6
DILATIONS = (1, 4, 16)
N_GROUPS = 3
HEADS_PER_GROUP = 8
HEAD_DIM = 64
GROUP_WIDTH = HEADS_PER_GROUP * HEAD_DIM
N_BUCKETS = 32
MAX_DISTANCE = 1024
NEG_INF = -1e30

HALF_KEYS = 64
LANES = 128
HALO = 16
ROW_TILE = 1024
QKV_TILE = 1024
COL_CHUNK = 256
DOWN_ROW_BLOCKS = 4
MAX_STRIDE = 4
Q_TILE = 128
K_TILE = Q_TILE + 2 * HALF_KEYS
ATTN_ROWS = 2048
LOG2E = math.log2(math.e)
V7X_VMEM_BYTES = 64 * 1024 * 1024
VMEM_LIMIT = V7X_VMEM_BYTES - 8 * 1024 * 1024

BF16 = jnp.bfloat16
F32 = jnp.float32


def _dot(a, b):
    return jnp.dot(a, b, preferred_element_type=F32)


def _rms(x, w):
    return x * lax.rsqrt(jnp.mean(x * x, axis=-1, keepdims=True) + EPS) * w


def _resident(shape):
    zeros = (0,) * len(shape)
    return pl.BlockSpec(shape, lambda *_: zeros, pipeline_mode=pl.Buffered(1))


def _seq_specs(seq, tile):
    per = tile // HALO
    last = seq // HALO - 1
    main = pl.BlockSpec((None, tile, D_MODEL), lambda b, i: (b, i, 0))
    prev = pl.BlockSpec((None, HALO, D_MODEL),
                        lambda b, i: (b, jnp.maximum(i * per - 1, 0), 0))
    nxt = pl.BlockSpec((None, HALO, D_MODEL),
                       lambda b, i: (b, jnp.minimum((i + 1) * per, last), 0))
    return main, prev, nxt


def _conv3(p, cw_ref, cols, tile):
    i = pl.program_id(1)
    rows = lax.broadcasted_iota(jnp.int32, (p.shape[0], 1), 0)
    outside = ((i == 0) & (rows < HALO)) | (
        (i == pl.num_programs(1) - 1) & (rows >= HALO + tile))
    p = jnp.where(outside, 0.0, p)
    n = p.shape[0]
    before = pltpu.roll(p, 1, axis=0)[HALO:HALO + tile]
    after = pltpu.roll(p, n - 1, axis=0)[HALO:HALO + tile]
    mid = p[HALO:HALO + tile]
    return (cw_ref[0:1, cols] * before + cw_ref[1:2, cols] * mid
            + cw_ref[2:3, cols] * after)


def _cast_plan(weights, grid):
    n_steps = grid[0] * grid[1]
    operands, in_specs, out_specs, out_shapes = [], [], [], []
    for w, layer in weights:
        _, rows, cols = w.shape
        n_blocks = next(n for n in (n_steps, n_steps // 2, n_steps // 4)
                        if rows % n == 0 and (rows // n) % HALO == 0)
        rep = n_steps // n_blocks

        def slab(b, i, rep=rep):
            return (b * grid[1] + i) // rep

        operands.append(w)
        in_specs.append(pl.BlockSpec((None, rows // n_blocks, cols),
                                     lambda b, i, layer=layer, slab=slab: (layer, slab(b, i), 0)))
        out_specs.append(pl.BlockSpec((rows // n_blocks, cols),
                                      lambda b, i, slab=slab: (slab(b, i), 0)))
        out_shapes.append(jax.ShapeDtypeStruct((rows, cols), BF16))
    return operands, in_specs, out_specs, out_shapes


def _cast_slabs(src_refs, dst_refs):
    for src, dst in zip(src_refs, dst_refs):
        dst[...] = src[...].astype(BF16)


def _mixer_kernel(*refs, n_cast):
    x_ref, xp_ref, xn_ref, nw_ref, win_ref, cw_ref, wout_ref = refs[:7]
    cast_src = refs[7:7 + n_cast]
    o_ref = refs[7 + n_cast]
    cast_dst = refs[8 + n_cast:8 + 2 * n_cast]
    hn_ref, a_ref = refs[8 + 2 * n_cast:]
    _cast_slabs(cast_src, cast_dst)
    tile = x_ref.shape[0]
    d = D_MODEL
    nw = nw_ref[...]
    hn_ref[0:HALO, :] = _rms(xp_ref[...], nw).astype(BF16)
    hn_ref[HALO:HALO + tile, :] = _rms(x_ref[...], nw).astype(BF16)
    hn_ref[HALO + tile:, :] = _rms(xn_ref[...], nw).astype(BF16)
    for j in range(d // COL_CHUNK):
        cols = slice(j * COL_CHUNK, (j + 1) * COL_CHUNK)
        gate = _dot(hn_ref[HALO:HALO + tile, :], win_ref[:, cols])
        c = _dot(hn_ref[...], win_ref[:, d + j * COL_CHUNK:d + (j + 1) * COL_CHUNK])
        h = _dot(hn_ref[...], win_ref[:, 2 * d + j * COL_CHUNK:2 * d + (j + 1) * COL_CHUNK])
        conv = _conv3(c * h, cw_ref, cols, tile)
        a_ref[:, cols] = (gate * conv).astype(BF16)
    o_ref[...] = x_ref[...] + _dot(a_ref[...], wout_ref[...])


def _ffn_kernel(*refs, has_branch, final_norm, n_cast):
    if has_branch:
        x_ref, xp_ref, xn_ref, y_ref, yp_ref, yn_ref = refs[:6]
        refs = refs[6:]
        tiles = [xp_ref[...] + yp_ref[...], x_ref[...] + y_ref[...], xn_ref[...] + yn_ref[...]]
    else:
        x_ref, xp_ref, xn_ref = refs[:3]
        refs = refs[3:]
        tiles = [xp_ref[...], x_ref[...], xn_ref[...]]
    nw_ref, wup_ref, cw_ref, cb_ref, wdown_ref, fw_ref = refs[:6]
    cast_src = refs[6:6 + n_cast]
    o_ref = refs[6 + n_cast]
    cast_dst = refs[7 + n_cast:7 + 2 * n_cast]
    hn_ref, a_ref = refs[7 + 2 * n_cast:]
    _cast_slabs(cast_src, cast_dst)
    tile = x_ref.shape[0]
    f = D_FF
    nw = nw_ref[...]
    hn_ref[0:HALO, :] = _rms(tiles[0], nw).astype(BF16)
    hn_ref[HALO:HALO + tile, :] = _rms(tiles[1], nw).astype(BF16)
    hn_ref[HALO + tile:, :] = _rms(tiles[2], nw).astype(BF16)
    for j in range(f // COL_CHUNK):
        cols = slice(j * COL_CHUNK, (j + 1) * COL_CHUNK)
        g = _dot(hn_ref[...], wup_ref[:, cols])
        u = _dot(hn_ref[HALO:HALO + tile, :],
                 wup_ref[:, f + j * COL_CHUNK:f + (j + 1) * COL_CHUNK])
        g = _conv3(g, cw_ref, cols, tile) + cb_ref[:, cols]
        act = g / (1.0 + jnp.exp(-g))
        a_ref[:, cols] = (act * u).astype(BF16)
    block = tile // DOWN_ROW_BLOCKS
    for rows in (slice(k * block, (k + 1) * block) for k in range(DOWN_ROW_BLOCKS)):
        y = x_ref[rows, :] + _dot(a_ref[rows, :], wdown_ref[...])
        if has_branch:
            y = y + y_ref[rows, :]
        if final_norm:
            y = _rms(y, fw_ref[...])
        o_ref[rows, :] = y


def _qkv_kernel(x_ref, nw_ref, w_ref, o0_ref, o1_ref, o2_ref, hn_ref, hp_ref):
    tile = x_ref.shape[0]
    gw = GROUP_WIDTH
    hn = _rms(x_ref[...], nw_ref[...])
    for c in range(D_MODEL // LANES):
        hn_ref[c] = hn[:, c * LANES:(c + 1) * LANES]
    for g, r in enumerate(DILATIONS):
        rows = tile // r
        for p in range(r):
            for c in range(D_MODEL // LANES):
                hp_ref[g, p * rows:(p + 1) * rows, c * LANES:(c + 1) * LANES] = (
                    hn_ref[c, pl.ds(p, rows, stride=r), :].astype(BF16))
    last_rows = tile // DILATIONS[-1]
    bits = None
    for p in range(DILATIONS[-1]):
        word = pltpu.bitcast(hp_ref[N_GROUPS - 1, p * last_rows:p * last_rows + HALO,
                                    D_MODEL - LANES:], jnp.int32)
        bits = word if bits is None else bits | word
    bits = lax.shift_right_logical(lax.shift_right_logical(bits, 16), 16)
    zero = bits.astype(F32)[0:1, :]
    q_scale = jnp.concatenate([zero] * (gw // LANES), axis=1) + HEAD_DIM ** -0.5 * LOG2E
    for g, (r, o_ref) in enumerate(zip(DILATIONS, (o0_ref, o1_ref, o2_ref))):
        rows = tile // r
        for which in range(3):
            col = (which * N_GROUPS + g) * gw
            res = _dot(hp_ref[g], w_ref[:, col:col + gw])
            if which == 0:
                res = res * (q_scale if g == 1 else HEAD_DIM ** -0.5 * LOG2E)
            for p in range(r):
                o_ref[p, :, which * gw:(which + 1) * gw] = (
                    res[p * rows:(p + 1) * rows].astype(BF16))


def _attn_kernel(q_ref, k_ref, v_ref, bias_ref, o_ref, m_ref, d_ref, *, phase_len,
                 key_origin=0):
    n_phase, n_rows, _ = q_ref.shape
    t = pl.program_id(2)
    lane = lax.broadcasted_iota(jnp.int32, (1, LANES), 1)
    low = lane < HEAD_DIM
    for ph in range(n_phase):
        for st in range(n_rows // Q_TILE):
            q0 = t * n_rows + st * Q_TILE
            first_key = jnp.clip(q0 - HALF_KEYS, 0, phase_len - K_TILE)
            variant = lax.shift_right_logical(q0 - first_key, HALF_KEYS.bit_length() - 1)
            start = pl.multiple_of(first_key - key_origin, HALF_KEYS)
            rows = slice(st * Q_TILE, (st + 1) * Q_TILE)
            m_all = jnp.zeros((Q_TILE, LANES), F32)
            d_all = jnp.ones((Q_TILE, LANES), F32)
            for hp in range(HEADS_PER_GROUP // 2):
                cols = slice(hp * LANES, (hp + 1) * LANES)
                q2 = q_ref[ph, rows, cols]
                k2 = k_ref[ph, pl.ds(start, K_TILE), cols]
                v2 = v_ref[ph, pl.ds(start, K_TILE), cols]
                zero = jnp.zeros_like(q2)
                qs = jnp.concatenate([jnp.where(low, q2, zero), jnp.where(low, zero, q2)], axis=0)
                s = lax.dot_general(qs, k2, (((1,), (1,)), ((), ())),
                                    preferred_element_type=F32)
                s = s + bias_ref[variant, hp]
                m = jnp.max(s, axis=-1, keepdims=True)
                e = jnp.exp2(s - m).astype(BF16)
                pv = _dot(e, jnp.concatenate([v2, jnp.ones_like(v2)], axis=1))
                num = pv[:, :LANES]
                den = pv[:, LANES:]
                o_ref[ph, rows, cols] = jnp.where(low, num[:Q_TILE], num[Q_TILE:]).astype(BF16)
                for half in range(2):
                    sel = lane == 2 * hp + half
                    part = slice(half * Q_TILE, (half + 1) * Q_TILE)
                    m_all = jnp.where(sel, m[part], m_all)
                    d_all = jnp.where(sel, den[part], d_all)
            m_ref[ph, rows, :] = m_all
            d_ref[ph, rows, :] = d_all


def _interleave(dst, src_of_phase, r, tile, tmp):
    rows = tile // r
    s = MAX_STRIDE
    if r <= s:
        for p in range(r):
            dst[pl.ds(p, rows, stride=r), :] = src_of_phase(p)
        return
    assert r // s <= s
    for p in range(r):
        hi, lo = divmod(p, s)
        tmp[pl.ds(lo * (tile // s) + hi, rows, stride=r // s), :] = src_of_phase(p)
    for lo in range(s):
        dst[pl.ds(lo, tile // s, stride=s), :] = tmp[lo * (tile // s):(lo + 1) * (tile // s), :]


def _combine_kernel(o0_ref, o1_ref, o2_ref, m0_ref, m1_ref, m2_ref,
                    d0_ref, d1_ref, d2_ref, expand_ref, w_ref,
                    out_ref, on_ref, mn_ref, dn_ref, tmp_ref, a_ref):
    tile = out_ref.shape[0]
    nchunk = GROUP_WIDTH // LANES
    groups = zip(DILATIONS, (o0_ref, o1_ref, o2_ref), (m0_ref, m1_ref, m2_ref),
                 (d0_ref, d1_ref, d2_ref))
    nums, maxes, dens = [], [], []
    for g, (r, o_ref, m_ref, d_ref) in enumerate(groups):
        if r == 1:
            nums.append([o_ref[0, :, c * LANES:(c + 1) * LANES].astype(F32)
                         for c in range(nchunk)])
            maxes.append(m_ref[0])
            dens.append(d_ref[0])
            continue
        for c in range(nchunk):
            _interleave(on_ref.at[g - 1, c],
                        lambda p, c=c: o_ref[p, :, c * LANES:(c + 1) * LANES].astype(F32),
                        r, tile, tmp_ref.at[c])
        _interleave(mn_ref.at[g - 1], lambda p: m_ref[p], r, tile, tmp_ref.at[nchunk])
        _interleave(dn_ref.at[g - 1], lambda p: d_ref[p], r, tile, tmp_ref.at[nchunk + 1])
        nums.append([on_ref[g - 1, c] for c in range(nchunk)])
        maxes.append(mn_ref[g - 1])
        dens.append(dn_ref[g - 1])
    m_max = jnp.maximum(jnp.maximum(maxes[0], maxes[1]), maxes[2])
    w = [jnp.exp2(m - m_max) for m in maxes]
    den = w[0] * dens[0] + w[1] * dens[1] + w[2] * dens[2]
    acc = [None] * nchunk
    for g in range(N_GROUPS):
        alpha = w[g] / den
        hi = alpha.astype(BF16)
        lo = (alpha - hi.astype(F32)).astype(BF16)
        wide = _dot(jnp.concatenate([hi, lo], axis=1), expand_ref[...])
        for c in range(nchunk):
            term = wide[:, c * LANES:(c + 1) * LANES] * nums[g][c]
            acc[c] = term if g == 0 else acc[c] + term
    for c in range(nchunk):
        a_ref[:, c * LANES:(c + 1) * LANES] = acc[c].astype(BF16)
    out_ref[...] = _dot(a_ref[...], w_ref[...])


def _attn_combine_kernel(q_ref, kp_ref, k_ref, kn_ref, vp_ref, v_ref, vn_ref, bias_ref,
                         o1_ref, o2_ref, m1_ref, m2_ref, d1_ref, d2_ref, expand_ref, w_ref,
                         out_ref, kbuf_ref, vbuf_ref, o0_ref, m0_ref, d0_ref,
                         on_ref, mn_ref, dn_ref, tmp_ref, a_ref, *, phase_len):
    tile = q_ref.shape[1]
    for buf, parts in ((kbuf_ref, (kp_ref, k_ref, kn_ref)), (vbuf_ref, (vp_ref, v_ref, vn_ref))):
        buf[0, 0:HALF_KEYS, :] = parts[0][0]
        buf[0, HALF_KEYS:HALF_KEYS + tile, :] = parts[1][0]
        buf[0, HALF_KEYS + tile:, :] = parts[2][0]
    _attn_kernel(q_ref, kbuf_ref, vbuf_ref, bias_ref, o0_ref, m0_ref, d0_ref,
                 phase_len=phase_len, key_origin=pl.program_id(2) * tile - HALF_KEYS)
    _combine_kernel(o0_ref, o1_ref, o2_ref, m0_ref, m1_ref, m2_ref, d0_ref, d1_ref, d2_ref,
                    expand_ref, w_ref, out_ref, on_ref, mn_ref, dn_ref, tmp_ref, a_ref)


def _params(n_axes=2):
    return pltpu.CompilerParams(
        dimension_semantics=("arbitrary",) * n_axes, vmem_limit_bytes=VMEM_LIMIT)


def _mixer(x, nw, w_in, cw, w_out, cast=()):
    b, s, d = x.shape
    grid = (b, s // ROW_TILE)
    main, prev, nxt = _seq_specs(s, ROW_TILE)
    c_ops, c_in, c_out, c_shapes = _cast_plan(cast, grid)
    return pl.pallas_call(
        functools.partial(_mixer_kernel, n_cast=len(cast)),
        grid=grid,
        in_specs=[main, prev, nxt, _resident((1, d)), _resident((d, 3 * d)),
                  _resident((3, d)), _resident((d, d))] + c_in,
        out_specs=[main] + c_out,
        out_shape=[jax.ShapeDtypeStruct(x.shape, F32)] + c_shapes,
        scratch_shapes=[pltpu.VMEM((ROW_TILE + 2 * HALO, d), BF16),
                        pltpu.VMEM((ROW_TILE, d), BF16)],
        compiler_params=_params(),
        name="short_conv_mixer",
    )(x, x, x, nw, w_in, cw, w_out, *c_ops)


def _ffn(x, branch, nw, w_up, cw, cb, w_down, fw, final_norm, cast=()):
    b, s, d = x.shape
    f = D_FF
    grid = (b, s // ROW_TILE)
    seq = list(_seq_specs(s, ROW_TILE))
    has_branch = branch is not None
    acts = [x, x, x] + ([branch, branch, branch] if has_branch else [])
    c_ops, c_in, c_out, c_shapes = _cast_plan(cast, grid)
    return pl.pallas_call(
        functools.partial(_ffn_kernel, has_branch=has_branch, final_norm=final_norm,
                          n_cast=len(cast)),
        grid=grid,
        in_specs=seq * (2 if has_branch else 1) + [
            _resident((1, d)), _resident((d, 2 * f)), _resident((3, f)), _resident((1, f)),
            _resident((f, d)), _resident((1, d))] + c_in,
        out_specs=[seq[0]] + c_out,
        out_shape=[jax.ShapeDtypeStruct(x.shape, F32)] + c_shapes,
        scratch_shapes=[pltpu.VMEM((ROW_TILE + 2 * HALO, d), BF16),
                        pltpu.VMEM((ROW_TILE, f), BF16)],
        compiler_params=_params(),
        name="conv_ffn_final" if final_norm else "conv_ffn",
    )(*acts, nw, w_up, cw, cb, w_down, fw, *c_ops)


def _qkv(x, nw, w):
    b, s, d = x.shape
    gw3 = 3 * GROUP_WIDTH
    out_shapes, out_specs = [], []
    for r in DILATIONS:
        out_shapes.append(jax.ShapeDtypeStruct((b, r, s // r, gw3), BF16))
        out_specs.append(pl.BlockSpec((None, r, QKV_TILE // r, gw3),
                                      lambda bi, i: (bi, 0, i, 0)))
    return pl.pallas_call(
        _qkv_kernel,
        grid=(b, s // QKV_TILE),
        in_specs=[pl.BlockSpec((None, QKV_TILE, d), lambda bi, i: (bi, i, 0)),
                  _resident((1, d)), _resident((d, N_GROUPS * gw3))],
        out_specs=out_specs,
        out_shape=out_shapes,
        scratch_shapes=[pltpu.VMEM((d // LANES, QKV_TILE, LANES), F32),
                        pltpu.VMEM((N_GROUPS, QKV_TILE, d), BF16)],
        compiler_params=_params(),
        name="qkv_proj",
    )(x, nw, w)


def _attention(qkv, bias, r):
    b, _, phase_len, _ = qkv.shape
    gw = GROUP_WIDTH
    n_rows = min(ATTN_ROWS, phase_len)
    n_phase = ATTN_ROWS // n_rows
    q_spec = pl.BlockSpec((None, n_phase, n_rows, gw), lambda bi, p, t: (bi, p, t, 0))
    k_spec = pl.BlockSpec((None, n_phase, phase_len, gw), lambda bi, p, t: (bi, p, 0, 1))
    v_spec = pl.BlockSpec((None, n_phase, phase_len, gw), lambda bi, p, t: (bi, p, 0, 2))
    stat_spec = pl.BlockSpec((None, n_phase, n_rows, LANES), lambda bi, p, t: (bi, p, t, 0))
    stat_shape = jax.ShapeDtypeStruct((b, r, phase_len, LANES), F32)
    return pl.pallas_call(
        functools.partial(_attn_kernel, phase_len=phase_len),
        grid=(b, r // n_phase, phase_len // n_rows),
        in_specs=[q_spec, k_spec, v_spec, _resident(bias.shape)],
        out_specs=[q_spec, stat_spec, stat_spec],
        out_shape=[jax.ShapeDtypeStruct((b, r, phase_len, gw), BF16), stat_shape, stat_shape],
        compiler_params=_params(3),
        name=f"local_attn_d{r}",
    )(qkv, qkv, qkv, bias)


def _attention_combine(qkv0, bias0, outs, maxes, dens, w_out):
    b, _, s, _ = qkv0.shape
    d = D_MODEL
    gw = GROUP_WIDTH
    tile = ROW_TILE
    rest = DILATIONS[1:]
    q_spec = pl.BlockSpec((None, 1, tile, gw), lambda bi, p, t: (bi, 0, t, 0))
    per = tile // HALF_KEYS
    last = s // HALF_KEYS - 1
    kv_specs = []
    for col in (1, 2):
        kv_specs += [
            pl.BlockSpec((None, 1, HALF_KEYS, gw),
                         lambda bi, p, t, col=col: (bi, 0, jnp.maximum(t * per - 1, 0), col)),
            pl.BlockSpec((None, 1, tile, gw), lambda bi, p, t, col=col: (bi, 0, t, col)),
            pl.BlockSpec((None, 1, HALF_KEYS, gw),
                         lambda bi, p, t, col=col: (bi, 0, jnp.minimum((t + 1) * per, last), col))]
    o_specs = [pl.BlockSpec((None, r, tile // r, gw), lambda bi, p, t: (bi, 0, t, 0))
               for r in rest]
    stat_specs = [pl.BlockSpec((None, r, tile // r, LANES), lambda bi, p, t: (bi, 0, t, 0))
                  for r in rest]
    expand = np.zeros((2 * LANES, gw), np.float32)
    for h in range(HEADS_PER_GROUP):
        expand[h, h * HEAD_DIM:(h + 1) * HEAD_DIM] = 1.0
        expand[LANES + h, h * HEAD_DIM:(h + 1) * HEAD_DIM] = 1.0
    return pl.pallas_call(
        functools.partial(_attn_combine_kernel, phase_len=s),
        grid=(b, 1, s // tile),
        in_specs=[q_spec] + kv_specs + [_resident(bias0.shape)] + o_specs + stat_specs
        + stat_specs + [_resident((2 * LANES, gw)), _resident((gw, d))],
        out_specs=pl.BlockSpec((None, tile, d), lambda bi, p, t: (bi, t, 0)),
        out_shape=jax.ShapeDtypeStruct((b, s, d), F32),
        scratch_shapes=[pltpu.VMEM((1, tile + 2 * HALF_KEYS, gw), BF16),
                        pltpu.VMEM((1, tile + 2 * HALF_KEYS, gw), BF16),
                        pltpu.VMEM((1, tile, gw), BF16),
                        pltpu.VMEM((1, tile, LANES), F32),
                        pltpu.VMEM((1, tile, LANES), F32),
                        pltpu.VMEM((N_GROUPS - 1, gw // LANES, tile, LANES), F32),
                        pltpu.VMEM((N_GROUPS - 1, tile, LANES), F32),
                        pltpu.VMEM((N_GROUPS - 1, tile, LANES), F32),
                        pltpu.VMEM((gw // LANES + 2, tile, LANES), F32),
                        pltpu.VMEM((tile, gw), BF16)],
        compiler_params=_params(3),
        name="local_attn_d1_combine_out",
    )(*([qkv0] * 7), bias0, *outs, *maxes, *dens, jnp.asarray(expand, BF16), w_out)


def _t5_bucket(rel):
    half = N_BUCKETS // 2
    max_exact = half // 2
    n = jnp.abs(rel)
    side = jnp.where(rel > 0, half, 0)
    nf = jnp.maximum(n, 1).astype(jnp.float32)
    large = max_exact + (jnp.log(nf / max_exact) / math.log(MAX_DISTANCE / max_exact)
                         * (half - max_exact)).astype(jnp.int32)
    large = jnp.minimum(large, half - 1)
    return side + jnp.where(n < max_exact, n, large)


def _bias_tiles(rel_bias, g):
    r = DILATIONS[g]
    heads = HEADS_PER_GROUP
    offs = r * jnp.arange(-HALF_KEYS, HALF_KEYS + 1, dtype=jnp.int32)
    onehot = _t5_bucket(offs)[:, None] == jnp.arange(N_BUCKETS)[None, :]
    table = rel_bias[:, g * heads:(g + 1) * heads].astype(F32)
    per_offset = jnp.sum(jnp.where(onehot[:, :, None], table[None], 0.0), axis=1) * LOG2E
    n = 4 * Q_TILE
    row = jnp.full((heads, n + 1), NEG_INF, F32)
    row = row.at[:, 2 * Q_TILE - HALF_KEYS:2 * Q_TILE + HALF_KEYS + 1].set(per_offset.T)
    skew = jnp.tile(row, (1, Q_TILE))[:, :Q_TILE * n].reshape(heads, Q_TILE, n)
    tiles = jnp.stack([skew[:, :, 2 * Q_TILE - HALF_KEYS * v:2 * Q_TILE - HALF_KEYS * v + K_TILE]
                       for v in range(3)])
    return tiles.reshape(3, heads // 2, 2 * Q_TILE, K_TILE)


def kernel(x, norm_w, conv_in, conv_w, conv_out, attn_qkv, attn_out, rel_bias,
           ffn_up, ffn_conv_w, ffn_conv_b, ffn_down, final_norm):
    d = D_MODEL
    fw = final_norm.reshape(1, d)

    x, w_up0, w_down0 = _mixer(
        x, norm_w[0, 0].reshape(1, d), conv_in[0].astype(BF16), conv_w[0],
        conv_out[0].astype(BF16), cast=((ffn_up, 0), (ffn_down, 0)))
    x, w_qkv, w_attn_out, w_up1, w_down1 = _ffn(
        x, None, norm_w[0, 1].reshape(1, d), w_up0, ffn_conv_w[0],
        ffn_conv_b[0].reshape(1, D_FF), w_down0, fw, False,
        cast=((attn_qkv, 0), (attn_out, 0), (ffn_up, 1), (ffn_down, 1)))

    qkvs = _qkv(x, norm_w[1, 0].reshape(1, d), w_qkv)
    outs, maxes, dens = [], [], []
    for g in range(1, N_GROUPS):
        o, m, den = _attention(qkvs[g], _bias_tiles(rel_bias, g), DILATIONS[g])
        outs.append(o)
        maxes.append(m)
        dens.append(den)
    branch = _attention_combine(qkvs[0], _bias_tiles(rel_bias, 0), outs, maxes, dens, w_attn_out)
    (x,) = _ffn(x, branch, norm_w[1, 1].reshape(1, d), w_up1, ffn_conv_w[1],
                ffn_conv_b[1].reshape(1, D_FF), w_down1, fw, True)
    return x
```

```python
import functools
import math

import jax
import jax.numpy as jnp
import numpy as np
from jax import lax
from jax.experimental import pallas as pl
from jax.experimental.pallas import tpu as pltpu

D_MODEL = 1024
D_FF = 2816
EPS = 1e-6
DILATIONS = (1, 4, 16)
N_GROUPS = 3
HEADS_PER_GROUP = 8
HEAD_DIM = 64
GROUP_WIDTH = HEADS_PER_GROUP * HEAD_DIM
N_BUCKETS = 32
MAX_DISTANCE = 1024
NEG_INF = -1e30

HALF_KEYS = 64
LANES = 128
HALO = 16
ROW_TILE = 1024
QKV_TILE = 1024
COL_CHUNK = 256
DOWN_ROW_BLOCKS = 4
MAX_STRIDE = 4
Q_TILE = 128
K_TILE = Q_TILE + 2 * HALF_KEYS
ATTN_ROWS = 2048
LOG2E = math.log2(math.e)
V7X_VMEM_BYTES = 64 * 1024 * 1024
VMEM_LIMIT = V7X_VMEM_BYTES - 8 * 1024 * 1024

BF16 = jnp.bfloat16
F32 = jnp.float32


def _dot(a, b):
    return jnp.dot(a, b, preferred_element_type=F32)


def _rms(x, w):
    return x * lax.rsqrt(jnp.mean(x * x, axis=-1, keepdims=True) + EPS) * w


def _resident(shape):
    zeros = (0,) * len(shape)
    return pl.BlockSpec(shape, lambda *_: zeros, pipeline_mode=pl.Buffered(1))


def _seq_specs(seq, tile):
    per = tile // HALO
    last = seq // HALO - 1
    main = pl.BlockSpec((None, tile, D_MODEL), lambda b, i: (b, i, 0))
    prev = pl.BlockSpec((None, HALO, D_MODEL),
                        lambda b, i: (b, jnp.maximum(i * per - 1, 0), 0))
    nxt = pl.BlockSpec((None, HALO, D_MODEL),
                       lambda b, i: (b, jnp.minimum((i + 1) * per, last), 0))
    return main, prev, nxt


def _conv3(p, cw_ref, cols, tile):
    i = pl.program_id(1)
    rows = lax.broadcasted_iota(jnp.int32, (p.shape[0], 1), 0)
    outside = ((i == 0) & (rows < HALO)) | (
        (i == pl.num_programs(1) - 1) & (rows >= HALO + tile))
    p = jnp.where(outside, 0.0, p)
    n = p.shape[0]
    before = pltpu.roll(p, 1, axis=0)[HALO:HALO + tile]
    after = pltpu.roll(p, n - 1, axis=0)[HALO:HALO + tile]
    mid = p[HALO:HALO + tile]
    return (cw_ref[0:1, cols] * before + cw_ref[1:2, cols] * mid
            + cw_ref[2:3, cols] * after)


def _cast_plan(weights, grid):
    n_steps = grid[0] * grid[1]
    operands, in_specs, out_specs, out_shapes = [], [], [], []
    for w, layer in weights:
        _, rows, cols = w.shape
        n_blocks = next(n for n in (n_steps, n_steps // 2, n_steps // 4)
                        if rows % n == 0 and (rows // n) % HALO == 0)
        rep = n_steps // n_blocks

        def slab(b, i, rep=rep):
            return (b * grid[1] + i) // rep

        operands.append(w)
        in_specs.append(pl.BlockSpec((None, rows // n_blocks, cols),
                                     lambda b, i, layer=layer, slab=slab: (layer, slab(b, i), 0)))
        out_specs.append(pl.BlockSpec((rows // n_blocks, cols),
                                      lambda b, i, slab=slab: (slab(b, i), 0)))
        out_shapes.append(jax.ShapeDtypeStruct((rows, cols), BF16))
    return operands, in_specs, out_specs, out_shapes


def _cast_slabs(src_refs, dst_refs):
    for src, dst in zip(src_refs, dst_refs):
        dst[...] = src[...].astype(BF16)


def _mixer_kernel(*refs, n_cast):
    x_ref, xp_ref, xn_ref, nw_ref, win_ref, cw_ref, wout_ref = refs[:7]
    cast_src = refs[7:7 + n_cast]
    o_ref = refs[7 + n_cast]
    cast_dst = refs[8 + n_cast:8 + 2 * n_cast]
    hn_ref, a_ref = refs[8 + 2 * n_cast:]
    _cast_slabs(cast_src, cast_dst)
    tile = x_ref.shape[0]
    d = D_MODEL
    nw = nw_ref[...]
    hn_ref[0:HALO, :] = _rms(xp_ref[...], nw).astype(BF16)
    hn_ref[HALO:HALO + tile, :] = _rms(x_ref[...], nw).astype(BF16)
    hn_ref[HALO + tile:, :] = _rms(xn_ref[...], nw).astype(BF16)
    for j in range(d // COL_CHUNK):
        cols = slice(j * COL_CHUNK, (j + 1) * COL_CHUNK)
        gate = _dot(hn_ref[HALO:HALO + tile, :], win_ref[:, cols])
        c = _dot(hn_ref[...], win_ref[:, d + j * COL_CHUNK:d + (j + 1) * COL_CHUNK])
        h = _dot(hn_ref[...], win_ref[:, 2 * d + j * COL_CHUNK:2 * d + (j + 1) * COL_CHUNK])
        conv = _conv3(c * h, cw_ref, cols, tile)
        a_ref[:, cols] = (gate * conv).astype(BF16)
    o_ref[...] = x_ref[...] + _dot(a_ref[...], wout_ref[...])


def _ffn_kernel(*refs, has_branch, final_norm, n_cast):
    if has_branch:
        x_ref, xp_ref, xn_ref, y_ref, yp_ref, yn_ref = refs[:6]
        refs = refs[6:]
        tiles = [xp_ref[...] + yp_ref[...], x_ref[...] + y_ref[...], xn_ref[...] + yn_ref[...]]
    else:
        x_ref, xp_ref, xn_ref = refs[:3]
        refs = refs[3:]
        tiles = [xp_ref[...], x_ref[...], xn_ref[...]]
    nw_ref, wup_ref, cw_ref, cb_ref, wdown_ref, fw_ref = refs[:6]
    cast_src = refs[6:6 + n_cast]
    o_ref = refs[6 + n_cast]
    cast_dst = refs[7 + n_cast:7 + 2 * n_cast]
    hn_ref, a_ref = refs[7 + 2 * n_cast:]
    _cast_slabs(cast_src, cast_dst)
    tile = x_ref.shape[0]
    f = D_FF
    nw = nw_ref[...]
    hn_ref[0:HALO, :] = _rms(tiles[0], nw).astype(BF16)
    hn_ref[HALO:HALO + tile, :] = _rms(tiles[1], nw).astype(BF16)
    hn_ref[HALO + tile:, :] = _rms(tiles[2], nw).astype(BF16)
    for j in range(f // COL_CHUNK):
        cols = slice(j * COL_CHUNK, (j + 1) * COL_CHUNK)
        g = _dot(hn_ref[...], wup_ref[:, cols])
        u = _dot(hn_ref[HALO:HALO + tile, :],
                 wup_ref[:, f + j * COL_CHUNK:f + (j + 1) * COL_CHUNK])
        g = _conv3(g, cw_ref, cols, tile) + cb_ref[:, cols]
        act = g / (1.0 + jnp.exp(-g))
        a_ref[:, cols] = (act * u).astype(BF16)
    block = tile // DOWN_ROW_BLOCKS
    for rows in (slice(k * block, (k + 1) * block) for k in range(DOWN_ROW_BLOCKS)):
        y = x_ref[rows, :] + _dot(a_ref[rows, :], wdown_ref[...])
        if has_branch:
            y = y + y_ref[rows, :]
        if final_norm:
            y = _rms(y, fw_ref[...])
        o_ref[rows, :] = y


def _qkv_kernel(x_ref, nw_ref, w_ref, o0_ref, o1_ref, o2_ref, hn_ref, hp_ref):
    tile = x_ref.shape[0]
    gw = GROUP_WIDTH
    hn = _rms(x_ref[...], nw_ref[...])
    for c in range(D_MODEL // LANES):
        hn_ref[c] = hn[:, c * LANES:(c + 1) * LANES]
    for g, (r, o_ref) in enumerate(zip(DILATIONS, (o0_ref, o1_ref, o2_ref))):
        rows = tile // r
        for p in range(r):
            for c in range(D_MODEL // LANES):
                hp_ref[g, p * rows:(p + 1) * rows, c * LANES:(c + 1) * LANES] = (
                    hn_ref[c, pl.ds(p, rows, stride=r), :].astype(BF16))
        for which in range(3):
            col = (which * N_GROUPS + g) * gw
            res = _dot(hp_ref[g], w_ref[:, col:col + gw])
            if which == 0:
                res = res * (HEAD_DIM ** -0.5 * LOG2E)
            for p in range(r):
                o_ref[p, :, which * gw:(which + 1) * gw] = (
                    res[p * rows:(p + 1) * rows].astype(BF16))


def _attn_kernel(q_ref, k_ref, v_ref, bias_ref, o_ref, m_ref, d_ref, *, phase_len,
                 key_origin=0):
    n_phase, n_rows, _ = q_ref.shape
    t = pl.program_id(2)
    lane = lax.broadcasted_iota(jnp.int32, (1, LANES), 1)
    low = lane < HEAD_DIM
    for ph in range(n_phase):
        for st in range(n_rows // Q_TILE):
            q0 = t * n_rows + st * Q_TILE
            first_key = jnp.clip(q0 - HALF_KEYS, 0, phase_len - K_TILE)
            variant = lax.shift_right_logical(q0 - first_key, HALF_KEYS.bit_length() - 1)
            start = pl.multiple_of(first_key - key_origin, HALF_KEYS)
            rows = slice(st * Q_TILE, (st + 1) * Q_TILE)
            m_all = jnp.zeros((Q_TILE, LANES), F32)
            d_all = jnp.ones((Q_TILE, LANES), F32)
            for hp in range(HEADS_PER_GROUP // 2):
                cols = slice(hp * LANES, (hp + 1) * LANES)
                q2 = q_ref[ph, rows, cols]
                k2 = k_ref[ph, pl.ds(start, K_TILE), cols]
                v2 = v_ref[ph, pl.ds(start, K_TILE), cols]
                zero = jnp.zeros_like(q2)
                qs = jnp.concatenate([jnp.where(low, q2, zero), jnp.where(low, zero, q2)], axis=0)
                s = lax.dot_general(qs, k2, (((1,), (1,)), ((), ())),
                                    preferred_element_type=F32)
                s = s + bias_ref[variant, hp]
                m = jnp.max(s, axis=-1, keepdims=True)
                e = jnp.exp2(s - m).astype(BF16)
                pv = _dot(e, jnp.concatenate([v2, jnp.ones_like(v2)], axis=1))
                num = pv[:, :LANES]
                den = pv[:, LANES:]
                o_ref[ph, rows, cols] = jnp.where(low, num[:Q_TILE], num[Q_TILE:]).astype(BF16)
                for half in range(2):
                    sel = lane == 2 * hp + half
                    part = slice(half * Q_TILE, (half + 1) * Q_TILE)
                    m_all = jnp.where(sel, m[part], m_all)
                    d_all = jnp.where(sel, den[part], d_all)
            m_ref[ph, rows, :] = m_all
            d_ref[ph, rows, :] = d_all


def _interleave(dst, src_of_phase, r, tile, tmp):
    rows = tile // r
    s = MAX_STRIDE
    if r <= s:
        for p in range(r):
            dst[pl.ds(p, rows, stride=r), :] = src_of_phase(p)
        return
    assert r // s <= s
    for p in range(r):
        hi, lo = divmod(p, s)
        tmp[pl.ds(lo * (tile // s) + hi, rows, stride=r // s), :] = src_of_phase(p)
    for lo in range(s):
        dst[pl.ds(lo, tile // s, stride=s), :] = tmp[lo * (tile // s):(lo + 1) * (tile // s), :]


def _combine_kernel(o0_ref, o1_ref, o2_ref, m0_ref, m1_ref, m2_ref,
                    d0_ref, d1_ref, d2_ref, expand_ref, w_ref,
                    out_ref, on_ref, mn_ref, dn_ref, tmp_ref, a_ref):
    tile = out_ref.shape[0]
    nchunk = GROUP_WIDTH // LANES
    groups = zip(DILATIONS, (o0_ref, o1_ref, o2_ref), (m0_ref, m1_ref, m2_ref),
                 (d0_ref, d1_ref, d2_ref))
    nums, maxes, dens = [], [], []
    for g, (r, o_ref, m_ref, d_ref) in enumerate(groups):
        if r == 1:
            nums.append([o_ref[0, :, c * LANES:(c + 1) * LANES].astype(F32)
                         for c in range(nchunk)])
            maxes.append(m_ref[0])
            dens.append(d_ref[0])
            continue
        for c in range(nchunk):
            _interleave(on_ref.at[g - 1, c],
                        lambda p, c=c: o_ref[p, :, c * LANES:(c + 1) * LANES].astype(F32),
                        r, tile, tmp_ref.at[c])
        _interleave(mn_ref.at[g - 1], lambda p: m_ref[p], r, tile, tmp_ref.at[nchunk])
        _interleave(dn_ref.at[g - 1], lambda p: d_ref[p], r, tile, tmp_ref.at[nchunk + 1])
        nums.append([on_ref[g - 1, c] for c in range(nchunk)])
        maxes.append(mn_ref[g - 1])
        dens.append(dn_ref[g - 1])
    m_max = jnp.maximum(jnp.maximum(maxes[0], maxes[1]), maxes[2])
    w = [jnp.exp2(m - m_max) for m in maxes]
    den = w[0] * dens[0] + w[1] * dens[1] + w[2] * dens[2]
    acc = [None] * nchunk
    for g in range(N_GROUPS):
        alpha = w[g] / den
        hi = alpha.astype(BF16)
        lo = (alpha - hi.astype(F32)).astype(BF16)
        wide = _dot(jnp.concatenate([hi, lo], axis=1), expand_ref[...])
        for c in range(nchunk):
            term = wide[:, c * LANES:(c + 1) * LANES] * nums[g][c]
            acc[c] = term if g == 0 else acc[c] + term
    for c in range(nchunk):
        a_ref[:, c * LANES:(c + 1) * LANES] = acc[c].astype(BF16)
    out_ref[...] = _dot(a_ref[...], w_ref[...])


def _attn_combine_kernel(q_ref, kp_ref, k_ref, kn_ref, vp_ref, v_ref, vn_ref, bias_ref,
                         o1_ref, o2_ref, m1_ref, m2_ref, d1_ref, d2_ref, expand_ref, w_ref,
                         out_ref, kbuf_ref, vbuf_ref, o0_ref, m0_ref, d0_ref,
                         on_ref, mn_ref, dn_ref, tmp_ref, a_ref, *, phase_len):
    tile = q_ref.shape[1]
    for buf, parts in ((kbuf_ref, (kp_ref, k_ref, kn_ref)), (vbuf_ref, (vp_ref, v_ref, vn_ref))):
        buf[0, 0:HALF_KEYS, :] = parts[0][0]
        buf[0, HALF_KEYS:HALF_KEYS + tile, :] = parts[1][0]
        buf[0, HALF_KEYS + tile:, :] = parts[2][0]
    _attn_kernel(q_ref, kbuf_ref, vbuf_ref, bias_ref, o0_ref, m0_ref, d0_ref,
                 phase_len=phase_len, key_origin=pl.program_id(2) * tile - HALF_KEYS)
    _combine_kernel(o0_ref, o1_ref, o2_ref, m0_ref, m1_ref, m2_ref, d0_ref, d1_ref, d2_ref,
                    expand_ref, w_ref, out_ref, on_ref, mn_ref, dn_ref, tmp_ref, a_ref)


def _params(n_axes=2):
    return pltpu.CompilerParams(
        dimension_semantics=("parallel",) + ("arbitrary",) * (n_axes - 1),
        vmem_limit_bytes=VMEM_LIMIT)


def _mixer(x, nw, w_in, cw, w_out, cast=()):
    b, s, d = x.shape
    grid = (b, s // ROW_TILE)
    main, prev, nxt = _seq_specs(s, ROW_TILE)
    c_ops, c_in, c_out, c_shapes = _cast_plan(cast, grid)
    return pl.pallas_call(
        functools.partial(_mixer_kernel, n_cast=len(cast)),
        grid=grid,
        in_specs=[main, prev, nxt, _resident((1, d)), _resident((d, 3 * d)),
                  _resident((3, d)), _resident((d, d))] + c_in,
        out_specs=[main] + c_out,
        out_shape=[jax.ShapeDtypeStruct(x.shape, F32)] + c_shapes,
        scratch_shapes=[pltpu.VMEM((ROW_TILE + 2 * HALO, d), BF16),
                        pltpu.VMEM((ROW_TILE, d), BF16)],
        compiler_params=_params(),
        name="short_conv_mixer",
    )(x, x, x, nw, w_in, cw, w_out, *c_ops)


def _ffn(x, branch, nw, w_up, cw, cb, w_down, fw, final_norm, cast=()):
    b, s, d = x.shape
    f = D_FF
    grid = (b, s // ROW_TILE)
    seq = list(_seq_specs(s, ROW_TILE))
    has_branch = branch is not None
    acts = [x, x, x] + ([branch, branch, branch] if has_branch else [])
    c_ops, c_in, c_out, c_shapes = _cast_plan(cast, grid)
    return pl.pallas_call(
        functools.partial(_ffn_kernel, has_branch=has_branch, final_norm=final_norm,
                          n_cast=len(cast)),
        grid=grid,
        in_specs=seq * (2 if has_branch else 1) + [
            _resident((1, d)), _resident((d, 2 * f)), _resident((3, f)), _resident((1, f)),
            _resident((f, d)), _resident((1, d))] + c_in,
        out_specs=[seq[0]] + c_out,
        out_shape=[jax.ShapeDtypeStruct(x.shape, F32)] + c_shapes,
        scratch_shapes=[pltpu.VMEM((ROW_TILE + 2 * HALO, d), BF16),
                        pltpu.VMEM((ROW_TILE, f), BF16)],
        compiler_params=_params(),
        name="conv_ffn_final" if final_norm else "conv_ffn",
    )(*acts, nw, w_up, cw, cb, w_down, fw, *c_ops)


def _qkv(x, nw, w):
    b, s, d = x.shape
    gw3 = 3 * GROUP_WIDTH
    out_shapes, out_specs = [], []
    for r in DILATIONS:
        out_shapes.append(jax.ShapeDtypeStruct((b, r, s // r, gw3), BF16))
        out_specs.append(pl.BlockSpec((None, r, QKV_TILE // r, gw3),
                                      lambda bi, i: (bi, 0, i, 0)))
    return pl.pallas_call(
        _qkv_kernel,
        grid=(b, s // QKV_TILE),
        in_specs=[pl.BlockSpec((None, QKV_TILE, d), lambda bi, i: (bi, i, 0)),
                  _resident((1, d)), _resident((d, N_GROUPS * gw3))],
        out_specs=out_specs,
        out_shape=out_shapes,
        scratch_shapes=[pltpu.VMEM((d // LANES, QKV_TILE, LANES), F32),
                        pltpu.VMEM((N_GROUPS, QKV_TILE, d), BF16)],
        compiler_params=_params(),
        name="qkv_proj",
    )(x, nw, w)


def _attention(qkv, bias, r):
    b, _, phase_len, _ = qkv.shape
    gw = GROUP_WIDTH
    n_rows = min(ATTN_ROWS, phase_len)
    n_phase = ATTN_ROWS // n_rows
    q_spec = pl.BlockSpec((None, n_phase, n_rows, gw), lambda bi, p, t: (bi, p, t, 0))
    k_spec = pl.BlockSpec((None, n_phase, phase_len, gw), lambda bi, p, t: (bi, p, 0, 1))
    v_spec = pl.BlockSpec((None, n_phase, phase_len, gw), lambda bi, p, t: (bi, p, 0, 2))
    stat_spec = pl.BlockSpec((None, n_phase, n_rows, LANES), lambda bi, p, t: (bi, p, t, 0))
    stat_shape = jax.ShapeDtypeStruct((b, r, phase_len, LANES), F32)
    return pl.pallas_call(
        functools.partial(_attn_kernel, phase_len=phase_len),
        grid=(b, r // n_phase, phase_len // n_rows),
        in_specs=[q_spec, k_spec, v_spec, _resident(bias.shape)],
        out_specs=[q_spec, stat_spec, stat_spec],
        out_shape=[jax.ShapeDtypeStruct((b, r, phase_len, gw), BF16), stat_shape, stat_shape],
        compiler_params=_params(3),
        name=f"local_attn_d{r}",
    )(qkv, qkv, qkv, bias)


def _attention_combine(qkv0, bias0, outs, maxes, dens, w_out):
    b, _, s, _ = qkv0.shape
    d = D_MODEL
    gw = GROUP_WIDTH
    tile = ROW_TILE
    rest = DILATIONS[1:]
    q_spec = pl.BlockSpec((None, 1, tile, gw), lambda bi, p, t: (bi, 0, t, 0))
    per = tile // HALF_KEYS
    last = s // HALF_KEYS - 1
    kv_specs = []
    for col in (1, 2):
        kv_specs += [
            pl.BlockSpec((None, 1, HALF_KEYS, gw),
                         lambda bi, p, t, col=col: (bi, 0, jnp.maximum(t * per - 1, 0), col)),
            pl.BlockSpec((None, 1, tile, gw), lambda bi, p, t, col=col: (bi, 0, t, col)),
            pl.BlockSpec((None, 1, HALF_KEYS, gw),
                         lambda bi, p, t, col=col: (bi, 0, jnp.minimum((t + 1) * per, last), col))]
    o_specs = [pl.BlockSpec((None, r, tile // r, gw), lambda bi, p, t: (bi, 0, t, 0))
               for r in rest]
    stat_specs = [pl.BlockSpec((None, r, tile // r, LANES), lambda bi, p, t: (bi, 0, t, 0))
                  for r in rest]
    expand = np.zeros((2 * LANES, gw), np.float32)
    for h in range(HEADS_PER_GROUP):
        expand[h, h * HEAD_DIM:(h + 1) * HEAD_DIM] = 1.0
        expand[LANES + h, h * HEAD_DIM:(h + 1) * HEAD_DIM] = 1.0
    return pl.pallas_call(
        functools.partial(_attn_combine_kernel, phase_len=s),
        grid=(b, 1, s // tile),
        in_specs=[q_spec] + kv_specs + [_resident(bias0.shape)] + o_specs + stat_specs
        + stat_specs + [_resident((2 * LANES, gw)), _resident((gw, d))],
        out_specs=pl.BlockSpec((None, tile, d), lambda bi, p, t: (bi, t, 0)),
        out_shape=jax.ShapeDtypeStruct((b, s, d), F32),
        scratch_shapes=[pltpu.VMEM((1, tile + 2 * HALF_KEYS, gw), BF16),
                        pltpu.VMEM((1, tile + 2 * HALF_KEYS, gw), BF16),
                        pltpu.VMEM((1, tile, gw), BF16),
                        pltpu.VMEM((1, tile, LANES), F32),
                        pltpu.VMEM((1, tile, LANES), F32),
                        pltpu.VMEM((N_GROUPS - 1, gw // LANES, tile, LANES), F32),
                        pltpu.VMEM((N_GROUPS - 1, tile, LANES), F32),
                        pltpu.VMEM((N_GROUPS - 1, tile, LANES), F32),
                        pltpu.VMEM((gw // LANES + 2, tile, LANES), F32),
                        pltpu.VMEM((tile, gw), BF16)],
        compiler_params=_params(3),
        name="local_attn_d1_combine_out",
    )(*([qkv0] * 7), bias0, *outs, *maxes, *dens, jnp.asarray(expand, BF16), w_out)


def _t5_bucket(rel):
    half = N_BUCKETS // 2
    max_exact = half // 2
    n = jnp.abs(rel)
    side = jnp.where(rel > 0, half, 0)
    nf = jnp.maximum(n, 1).astype(jnp.float32)
    large = max_exact + (jnp.log(nf / max_exact) / math.log(MAX_DISTANCE / max_exact)
                         * (half - max_exact)).astype(jnp.int32)
    large = jnp.minimum(large, half - 1)
    return side + jnp.where(n < max_exact, n, large)


def _bias_tiles(rel_bias, g):
    r = DILATIONS[g]
    heads = HEADS_PER_GROUP
    offs = r * jnp.arange(-HALF_KEYS, HALF_KEYS + 1, dtype=jnp.int32)
    onehot = _t5_bucket(offs)[:, None] == jnp.arange(N_BUCKETS)[None, :]
    table = rel_bias[:, g * heads:(g + 1) * heads].astype(F32)
    per_offset = jnp.sum(jnp.where(onehot[:, :, None], table[None], 0.0), axis=1) * LOG2E
    n = 4 * Q_TILE
    row = jnp.full((heads, n + 1), NEG_INF, F32)
    row = row.at[:, 2 * Q_TILE - HALF_KEYS:2 * Q_TILE + HALF_KEYS + 1].set(per_offset.T)
    skew = jnp.tile(row, (1, Q_TILE))[:, :Q_TILE * n].reshape(heads, Q_TILE, n)
    tiles = jnp.stack([skew[:, :, 2 * Q_TILE - HALF_KEYS * v:2 * Q_TILE - HALF_KEYS * v + K_TILE]
                       for v in range(3)])
    return tiles.reshape(3, heads // 2, 2 * Q_TILE, K_TILE)


def kernel(x, norm_w, conv_in, conv_w, conv_out, attn_qkv, attn_out, rel_bias,
           ffn_up, ffn_conv_w, ffn_conv_b, ffn_down, final_norm):
    d = D_MODEL
    fw = final_norm.reshape(1, d)

    x, w_up0, w_down0 = _mixer(
        x, norm_w[0, 0].reshape(1, d), conv_in[0].astype(BF16), conv_w[0],
        conv_out[0].astype(BF16), cast=((ffn_up, 0), (ffn_down, 0)))
    x, w_qkv, w_attn_out, w_up1, w_down1 = _ffn(
        x, None, norm_w[0, 1].reshape(1, d), w_up0, ffn_conv_w[0],
        ffn_conv_b[0].reshape(1, D_FF), w_down0, fw, False,
        cast=((attn_qkv, 0), (attn_out, 0), (ffn_up, 1), (ffn_down, 1)))

    qkvs = _qkv(x, norm_w[1, 0].reshape(1, d), w_qkv)
    outs, maxes, dens = [], [], []
    for g in range(1, N_GROUPS):
        o, m, den = _attention(qkvs[g], _bias_tiles(rel_bias, g), DILATIONS[g])
        outs.append(o)
        maxes.append(m)
        dens.append(den)
    branch = _attention_combine(qkvs[0], _bias_tiles(rel_bias, 0), outs, maxes, dens, w_attn_out)
    (x,) = _ffn(x, branch, norm_w[1, 1].reshape(1, d), w_up1, ffn_conv_w[1],
                ffn_conv_b[1].reshape(1, D_FF), w_down1, fw, True)
    return x
```
